```python
import math
import jax
import jax.numpy as jnp
from jax import lax
import numpy as np

D_MODEL = 1024
BATCH = 2
SEQ = 8192
DEPTH = 2

GRID_W = 64
CTX_LEN = 256
EPS = 1e-6

RET_HEADS = 8
RET_DV = D_MODEL // RET_HEADS
RET_DK = RET_DV // 2
RET_CHUNK = 128
ROPE_BASE = 10000.0

CONF_WIDTH = D_MODEL // 2
CONF_TAPS = 31

HY_WIDTH = D_MODEL // 2
HY_ORDER = 2
HY_SHORT_TAPS = 3
HY_BANDS = 16
HY_EMB = 2 * HY_BANDS + 1
HY_HIDDEN = 64
HY_DECAY_SHORT_PCT = 0.3
HY_DECAY_LONG_PCT = 1.5
HY_DECAY_TARGET = 1e-2

PEER_HEADS = 8
PEER_NKEYS = 128
PEER_EXPERTS = PEER_NKEYS * PEER_NKEYS
PEER_DKEY = D_MODEL // PEER_HEADS
PEER_TOPK = 16
PEER_TOKEN_BLOCK = 128

N_BRANCH = 3
OFF_Q = 0
OFF_K = OFF_Q + RET_HEADS * RET_DK
OFF_V = OFF_K + RET_HEADS * RET_DK
OFF_G = OFF_V + RET_HEADS * RET_DV
OFF_CONF = OFF_G + RET_HEADS * RET_DV
OFF_HY = OFF_CONF + 2 * CONF_WIDTH
OFF_GATE = OFF_HY + (HY_ORDER + 1) * HY_WIDTH
IN_COLS = OFF_GATE + N_BRANCH * D_MODEL

kernel_name = "hybrid_retention_conformer_hyena_peer_dit"


def rms_norm(x, gain):
    xf = x.astype(jnp.float32)
    y = xf * lax.rsqrt(jnp.mean(xf * xf, axis=-1, keepdims=True) + EPS)
    return (y * gain.astype(jnp.float32)).astype(x.dtype)


def modulate(h, shift, scale):
    return h * (1 + scale) + shift


def adaln(cvec, w, b):
    m = (jax.nn.silu(cvec) @ w + b)[:, None, :]
    return jnp.split(m, 6, axis=-1)


def depthwise_conv(x, w, b):
    taps = w.shape[0]
    left = (taps - 1) // 2
    y = lax.conv_general_dilated(
        x, w[:, None, :].astype(x.dtype), window_strides=(1,),
        padding=[(left, taps - 1 - left)],
        dimension_numbers=('NWC', 'WIO', 'NWC'),
        feature_group_count=x.shape[-1])
    return y + b.astype(x.dtype)


def split_heads(a, d):
    return a.astype(jnp.float32).reshape(a.shape[0], a.shape[1], -1, d)


def grid_rope_angles(rows):
    rr, cc = jnp.meshgrid(jnp.arange(rows, dtype=jnp.float32),
                          jnp.arange(GRID_W, dtype=jnp.float32), indexing='ij')
    row = rr.reshape(-1)
    col = cc.reshape(-1)
    n_freq = RET_DK // 4
    inv = ROPE_BASE ** (-jnp.arange(n_freq, dtype=jnp.float32) / n_freq)
    ang = jnp.concatenate([row[:, None] * inv, col[:, None] * inv], axis=-1)
    return jnp.cos(ang), jnp.sin(ang)


def apply_rope(x, cos, sin):
    xp = x.reshape(x.shape[:-1] + (-1, 2))
    x0, x1 = xp[..., 0], xp[..., 1]
    c = cos[None, :, None, :]
    s = sin[None, :, None, :]
    return jnp.stack([x0 * c - x1 * s, x0 * s + x1 * c], axis=-1).reshape(x.shape)


def retention_chunkwise(q, k, v, log_g, s0):
    B, L, H, dk = q.shape
    dv = v.shape[-1]
    C = RET_CHUNK
    n = L // C
    qc = q.reshape(B, n, C, H, dk)
    kc = k.reshape(B, n, C, H, dk)
    vc = v.reshape(B, n, C, H, dv)
    idx = jnp.arange(C, dtype=jnp.float32)
    diff = idx[:, None] - idx[None, :]
    dmask = jnp.where(diff[None] >= 0,
                      jnp.exp(jnp.maximum(diff, 0.0)[None] * log_g[:, None, None]), 0.0)
    scores = jnp.einsum('bnihd,bnjhd->bnhij', qc, kc) * dmask
    intra = jnp.einsum('bnhij,bnjhe->bnihe', scores, vc)
    k_dec = kc * jnp.exp((C - 1 - idx)[:, None] * log_g[None, :])[:, :, None]
    chunk_kv = jnp.einsum('bnjhd,bnjhe->nbhde', k_dec, vc)
    g_chunk = jnp.exp(C * log_g)[None, :, None, None]

    def step(s, kv):
        return g_chunk * s + kv, s

    _, s_before = lax.scan(step, s0, chunk_kv)
    q_dec = qc * jnp.exp((idx + 1)[:, None] * log_g[None, :])[:, :, None]
    cross = jnp.einsum('bnihd,nbhde->bnihe', q_dec, s_before)
    return (intra + cross).reshape(B, L, H, dv)


def retention_final_states(k, v, log_g):
    L = k.shape[1]
    t = jnp.arange(L, dtype=jnp.float32)
    wf = jnp.exp((L - 1 - t)[:, None] * log_g[0][None, :])
    wb = jnp.exp(t[:, None] * log_g[1][None, :])
    s_f = jnp.einsum('blhd,lh,blhe->bhde', k, wf, v)
    s_b = jnp.einsum('blhd,lh,blhe->bhde', k, wb, v)
    return s_f, s_b


def bidir_retention(q, k, v, log_g, s0_f, s0_b):
    o_f = retention_chunkwise(q, k, v, log_g[0], s0_f)
    fl = lambda a: jnp.flip(a, axis=1)
    o_b = fl(retention_chunkwise(fl(q), fl(k), fl(v), log_g[1], s0_b))
    return o_f + o_b


def retention_branch(q, k, v, g, s0_f, s0_b, lp):
    o = bidir_retention(q, k, v, lp['log_g'], s0_f, s0_b)
    mu = jnp.mean(o, axis=-1, keepdims=True)
    var = jnp.mean(jnp.square(o - mu), axis=-1, keepdims=True)
    o = (o - mu) * lax.rsqrt(var + EPS)
    o = o.reshape(o.shape[0], o.shape[1], -1) * lp['ret_gn_g'].astype(jnp.float32)
    return (o.astype(g.dtype) * jax.nn.silu(g)) @ lp['w_br_ret']


def conformer_branch(pc, lp):
    a, b = jnp.split(pc, 2, axis=-1)
    h = depthwise_conv(a * jax.nn.sigmoid(b), lp['conf_dw_w'], lp['conf_dw_b'])
    hf = h.astype(jnp.float32)
    mu = jnp.mean(hf, axis=-1, keepdims=True)
    var = jnp.mean(jnp.square(hf - mu), axis=-1, keepdims=True)
    hn = (hf - mu) * lax.rsqrt(var + EPS) * lp['conf_ln_g'].astype(jnp.float32) + lp['conf_ln_b'].astype(jnp.float32)
    return jax.nn.silu(hn).astype(pc.dtype) @ lp['w_br_conf']


def hyena_filters(L, lp):
    f32 = jnp.float32
    t = jnp.arange(L, dtype=f32)
    t_norm = t / max(L - 1, 1)
    w_ang = 2.0 * math.pi * t / L
    bands = jnp.linspace(1e-4, HY_BANDS - 1, HY_BANDS, dtype=f32)
    feat = jnp.concatenate([t_norm[:, None], jnp.cos(w_ang[:, None] * bands),
                            -jnp.sin(w_ang[:, None] * bands)], axis=-1)
    freq = lp['hy_freq'].astype(f32)
    h = jnp.sin(freq * (feat @ lp['hy_w1'].astype(f32) + lp['hy_b1'].astype(f32)))
    h = jnp.sin(freq * (h @ lp['hy_w2'].astype(f32) + lp['hy_b2'].astype(f32)))
    h = (h @ lp['hy_w3'].astype(f32) + lp['hy_b3'].astype(f32)).reshape(L, HY_ORDER, 2, HY_WIDTH)
    deltas = jnp.abs(jnp.linspace(math.log(HY_DECAY_TARGET) / HY_DECAY_LONG_PCT,
                                  math.log(HY_DECAY_TARGET) / HY_DECAY_SHORT_PCT, HY_WIDTH, dtype=f32))
    window = jnp.exp(-t_norm[:, None] * deltas[None, :])
    h = h * window[:, None, None, :]
    h_fwd = h[:, :, 0]
    h_bwd = h[:, :, 1]
    taps = jnp.concatenate([h_fwd, jnp.zeros_like(h_fwd[:1]), jnp.flip(h_bwd[1:], axis=0)], axis=0)
    taps = taps * lax.rsqrt(jnp.sum(taps * taps, axis=0, keepdims=True) + EPS)
    return jnp.fft.rfft(taps, axis=0)


def hyena_branch(ph, filt_fft, lp):
    L = ph.shape[1]
    ph = depthwise_conv(ph, lp['hy_short_w'], lp['hy_short_b'])
    v, x1, x2 = jnp.split(ph, 3, axis=-1)
    z = v.astype(jnp.float32)
    bias = lp['hy_bias'].astype(jnp.float32)
    for o, gate in enumerate((x1, x2)):
        zf = jnp.fft.rfft(z, n=2 * L, axis=1)
        y = jnp.fft.irfft(zf * filt_fft[None, :, o, :], n=2 * L, axis=1)[:, :L]
        z = gate.astype(jnp.float32) * (y + z * bias[o])
    return z.astype(ph.dtype) @ lp['w_br_hy']


def mixer_block(p, s0_f, s0_b, rope, filt_fft, lp):
    q = split_heads(p[..., OFF_Q:OFF_K], RET_DK)
    k = split_heads(p[..., OFF_K:OFF_V], RET_DK) * RET_DK ** -0.5
    v = split_heads(p[..., OFF_V:OFF_G], RET_DV)
    if rope is not None:
        q = apply_rope(q, rope[0], rope[1])
        k = apply_rope(k, rope[0], rope[1])
    b_ret = retention_branch(q, k, v, p[..., OFF_G:OFF_CONF], s0_f, s0_b, lp)
    b_conf = conformer_branch(p[..., OFF_CONF:OFF_HY], lp)
    b_hy = hyena_branch(p[..., OFF_HY:OFF_GATE], filt_fft, lp)
    g_ret, g_conf, g_hy = jnp.split(jax.nn.sigmoid(p[..., OFF_GATE:]), N_BRANCH, axis=-1)
    return (g_ret * b_ret + g_conf * b_conf + g_hy * b_hy) @ lp['w_out']


def peer_ffn(h, w_q, sub_keys, u_tab, v_tab):
    B, L, D = h.shape
    T = B * L
    ht = h.reshape(T, D)
    q = (ht @ w_q).astype(jnp.float32).reshape(T, PEER_HEADS, 2, PEER_DKEY // 2)
    s = jnp.einsum('thpd,hpkd->thpk', q, sub_keys.astype(jnp.float32))
    s_top, i_top = lax.top_k(s, PEER_TOPK)
    cand_s = s_top[:, :, 0, :, None] + s_top[:, :, 1, None, :]
    cand_i = i_top[:, :, 0, :, None] * PEER_NKEYS + i_top[:, :, 1, None, :]
    best_s, best_j = lax.top_k(cand_s.reshape(T, PEER_HEADS, -1), PEER_TOPK)
    experts = jnp.take_along_axis(cand_i.reshape(T, PEER_HEADS, -1), best_j, axis=-1)
    gates = jax.nn.softmax(best_s, axis=-1)
    n_blk = T // PEER_TOKEN_BLOCK

    def expert_block(args):
        hb, eb, gb = args
        a = jnp.einsum('tkd,td->tk', u_tab[eb], hb)
        w = (jax.nn.gelu(a.astype(jnp.float32)) * gb).astype(hb.dtype)
        return jnp.einsum('tk,tkd->td', w, v_tab[eb])

    out = lax.map(expert_block, (ht.reshape(n_blk, PEER_TOKEN_BLOCK, D),
                                 experts.reshape(n_blk, PEER_TOKEN_BLOCK, -1),
                                 gates.reshape(n_blk, PEER_TOKEN_BLOCK, -1)))
    return out.reshape(B, L, D)


def setup_inputs(seed: int = 0) -> dict:
    key = jax.random.key(seed)
    ks = iter(jax.random.split(key, 40))

    def nrm(shape, scale):
        return jax.random.normal(next(ks), shape, jnp.float32) * scale

    D = D_MODEL
    hh = np.arange(RET_HEADS, dtype=np.float64)
    gamma0 = 1.0 - 2.0 ** (-5.0 - hh)
    base_logit = jnp.asarray(np.log(gamma0) - np.log1p(-gamma0), dtype=jnp.float32)
    return {
        'x': nrm((BATCH, SEQ, D), 1.0),
        'c': nrm((BATCH, D), 1.0),
        'ctx': nrm((BATCH, CTX_LEN, D), 1.0),
        'c_ctx': nrm((D,), 1.0),
        'ada_w': nrm((DEPTH, D, 6 * D), 0.5 * D ** -0.5),
        'ada_b': nrm((DEPTH, 6 * D), 0.02),
        'norm1_g': 1.0 + nrm((DEPTH, D), 0.02),
        'norm2_g': 1.0 + nrm((DEPTH, D), 0.02),
        'w_in': nrm((DEPTH, D, IN_COLS), D ** -0.5),
        'ret_decay_logit': base_logit + nrm((DEPTH, 2, RET_HEADS), 0.05),
        'ret_gn_g': 1.0 + nrm((DEPTH, RET_HEADS * RET_DV), 0.02),
        'w_br_ret': nrm((DEPTH, RET_HEADS * RET_DV, D), (RET_HEADS * RET_DV) ** -0.5),
        'conf_dw_w': nrm((DEPTH, CONF_TAPS, CONF_WIDTH), CONF_TAPS ** -0.5),
        'conf_dw_b': nrm((DEPTH, CONF_WIDTH), 0.02),
        'conf_ln_g': 1.0 + nrm((DEPTH, CONF_WIDTH), 0.02),
        'conf_ln_b': nrm((DEPTH, CONF_WIDTH), 0.02),
        'w_br_conf': nrm((DEPTH, CONF_WIDTH, D), CONF_WIDTH ** -0.5),
        'hy_short_w': nrm((DEPTH, HY_SHORT_TAPS, (HY_ORDER + 1) * HY_WIDTH), HY_SHORT_TAPS ** -0.5),
        'hy_short_b': nrm((DEPTH, (HY_ORDER + 1) * HY_WIDTH), 0.02),
        'hy_w1': nrm((DEPTH, HY_EMB, HY_HIDDEN), HY_EMB ** -0.5),
        'hy_b1': nrm((DEPTH, HY_HIDDEN), 0.02),
        'hy_freq': 1.0 + nrm((DEPTH, HY_HIDDEN), 0.1),
        'hy_w2': nrm((DEPTH, HY_HIDDEN, HY_HIDDEN), HY_HIDDEN ** -0.5),
        'hy_b2': nrm((DEPTH, HY_HIDDEN), 0.02),
        'hy_w3': nrm((DEPTH, HY_HIDDEN, 2 * HY_ORDER * HY_WIDTH), HY_HIDDEN ** -0.5),
        'hy_b3': nrm((DEPTH, 2 * HY_ORDER * HY_WIDTH), 0.02),
        'hy_bias': nrm((DEPTH, HY_ORDER, HY_WIDTH), 0.5),
        'w_br_hy': nrm((DEPTH, HY_WIDTH, D), HY_WIDTH ** -0.5),
        'w_out': nrm((DEPTH, D, D), D ** -0.5),
        'peer_wq': nrm((DEPTH, D, PEER_HEADS * PEER_DKEY), D ** -0.5),
        'peer_keys': nrm((DEPTH, PEER_HEADS, 2, PEER_NKEYS, PEER_DKEY // 2), (PEER_DKEY // 2) ** -0.5),
        'peer_u': nrm((DEPTH, PEER_EXPERTS, D), D ** -0.5),
        'peer_v': nrm((DEPTH, PEER_EXPERTS, D), 1.0),
        'final_g': 1.0 + nrm((D,), 0.02),
    }


def reference(x, c, ctx, c_ctx, ada_w, ada_b, norm1_g, norm2_g, w_in, ret_decay_logit, ret_gn_g,
              w_br_ret, conf_dw_w, conf_dw_b, conf_ln_g, conf_ln_b, w_br_conf, hy_short_w, hy_short_b,
              hy_w1, hy_b1, hy_freq, hy_w2, hy_b2, hy_w3, hy_b3, hy_bias, w_br_hy, w_out, peer_wq,
              peer_keys, peer_u, peer_v, final_g):
    L = x.shape[1]
    Lc = ctx.shape[1]
    ROWS = L // GRID_W
    rope = grid_rope_angles(ROWS)
    for l in range(DEPTH):
        last = l == DEPTH - 1
        lp = {
            'log_g': jax.nn.log_sigmoid(ret_decay_logit[l].astype(jnp.float32)),
            'ret_gn_g': ret_gn_g[l], 'w_br_ret': w_br_ret[l],
            'conf_dw_w': conf_dw_w[l], 'conf_dw_b': conf_dw_b[l],
            'conf_ln_g': conf_ln_g[l], 'conf_ln_b': conf_ln_b[l], 'w_br_conf': w_br_conf[l],
            'hy_short_w': hy_short_w[l], 'hy_short_b': hy_short_b[l],
            'hy_w1': hy_w1[l], 'hy_b1': hy_b1[l], 'hy_freq': hy_freq[l],
            'hy_w2': hy_w2[l], 'hy_b2': hy_b2[l], 'hy_w3': hy_w3[l], 'hy_b3': hy_b3[l],
            'hy_bias': hy_bias[l], 'w_br_hy': w_br_hy[l], 'w_out': w_out[l],
        }
        mod = adaln(c, ada_w[l], ada_b[l])
        mod_c = adaln(c_ctx[None, :], ada_w[l], ada_b[l])

        u_c = modulate(rms_norm(ctx, norm1_g[l]), mod_c[0], mod_c[1])
        if last:
            pk_c = u_c @ w_in[l][:, OFF_K:OFF_V]
            pv_c = u_c @ w_in[l][:, OFF_V:OFF_G]
        else:
            p_c = u_c @ w_in[l]
            pk_c = p_c[..., OFF_K:OFF_V]
            pv_c = p_c[..., OFF_V:OFF_G]
        s_f, s_b = retention_final_states(split_heads(pk_c, RET_DK) * RET_DK ** -0.5,
                                          split_heads(pv_c, RET_DV), lp['log_g'])

        u = modulate(rms_norm(x, norm1_g[l]), mod[0], mod[1])
        p = u @ w_in[l]
        x = x + mod[2] * mixer_block(p, s_f, s_b, rope, hyena_filters(L, lp), lp)

        x = x + mod[5] * peer_ffn(modulate(rms_norm(x, norm2_g[l]), mod[3], mod[4]),
                                  peer_wq[l], peer_keys[l], peer_u[l], peer_v[l])

        if not last:
            zero = jnp.zeros_like(s_f)
            ctx = ctx + mod_c[2] * mixer_block(p_c, zero, zero, None, hyena_filters(Lc, lp), lp)
            ctx = ctx + mod_c[5] * peer_ffn(modulate(rms_norm(ctx, norm2_g[l]), mod_c[3], mod_c[4]),
                                            peer_wq[l], peer_keys[l], peer_u[l], peer_v[l])
    return rms_norm(x, final_g)
```

```python
import functools
import math

import jax
import jax.numpy as jnp
import numpy as np
from jax import lax
from jax.experimental import pallas as pl
from jax.experimental.pallas import tpu as pltpu

D_MODEL = 1024
GRID_W = 64
EPS = 1e-6

RET_HEADS = 8
RET_DV = D_MODEL // RET_HEADS
RET_DK = RET_DV // 2
RET_CHUNK = 128
ROPE_BASE = 10000.0

CONF_WIDTH = D_MODEL // 2
CONF_TAPS = 31

HY_WIDTH = D_MODEL // 2
HY_ORDER = 2
HY_SHORT_TAPS = 3
HY_BANDS = 16
HY_EMB = 2 * HY_BANDS + 1
HY_HIDDEN = 64
HY_DECAY_SHORT_PCT = 0.3
HY_DECAY_LONG_PCT = 1.5
HY_DECAY_TARGET = 1e-2

PEER_HEADS = 8
PEER_NKEYS = 128
PEER_DKEY = D_MODEL // PEER_HEADS
PEER_TOPK = 16
PEER_TOKEN_BLOCK = 128

N_BRANCH = 3
OFF_Q = 0
OFF_K = OFF_Q + RET_HEADS * RET_DK
OFF_V = OFF_K + RET_HEADS * RET_DK
OFF_G = OFF_V + RET_HEADS * RET_DV
OFF_CONF = OFF_G + RET_HEADS * RET_DV
OFF_HY = OFF_CONF + 2 * CONF_WIDTH
OFF_GATE = OFF_HY + (HY_ORDER + 1) * HY_WIDTH
IN_COLS = OFF_GATE + N_BRANCH * D_MODEL


def _matmul_body(x_ref, w_ref, o_ref):
    o_ref[...] = jnp.dot(x_ref[...].astype(jnp.bfloat16), w_ref[...].astype(jnp.bfloat16),
                         preferred_element_type=jnp.float32)


def _pick_tile(n, pref):
    t = min(n, pref)
    while n % t:
        t //= 2
    return t


def matmul(x, w):
    lead = x.shape[:-1]
    x2 = x.reshape(-1, x.shape[-1])
    M, K = x2.shape
    N = w.shape[1]
    tm = _pick_tile(M, 512)
    tn = _pick_tile(N, 512)
    out = pl.pallas_call(
        _matmul_body,
        grid=(M // tm, N // tn),
        in_specs=[pl.BlockSpec((tm, K), lambda i, j: (i, 0)),
                  pl.BlockSpec((K, tn), lambda i, j: (0, j))],
        out_specs=pl.BlockSpec((tm, tn), lambda i, j: (i, j)),
        out_shape=jax.ShapeDtypeStruct((M, N), jnp.float32),
        compiler_params=pltpu.CompilerParams(dimension_semantics=("parallel", "parallel")),
        name="matmul",
    )(x2, w)
    return out.reshape(lead + (N,))


def rms_norm(x, gain):
    xf = x.astype(jnp.float32)
    y = xf * lax.rsqrt(jnp.mean(xf * xf, axis=-1, keepdims=True) + EPS)
    return (y * gain.astype(jnp.float32)).astype(x.dtype)


def modulate(h, shift, scale):
    return h * (1 + scale) + shift


def adaln(cvec, w, b):
    m = (jax.nn.silu(cvec) @ w + b)[:, None, :]
    return jnp.split(m, 6, axis=-1)


def depthwise_conv(x, w, b):
    taps = w.shape[0]
    left = (taps - 1) // 2
    y = lax.conv_general_dilated(
        x, w[:, None, :].astype(x.dtype), window_strides=(1,),
        padding=[(left, taps - 1 - left)],
        dimension_numbers=('NWC', 'WIO', 'NWC'),
        feature_group_count=x.shape[-1])
    return y + b.astype(x.dtype)


def split_heads(a, d):
    return a.astype(jnp.float32).reshape(a.shape[0], a.shape[1], -1, d)


def grid_rope_angles(rows):
    rr, cc = jnp.meshgrid(jnp.arange(rows, dtype=jnp.float32),
                          jnp.arange(GRID_W, dtype=jnp.float32), indexing='ij')
    row = rr.reshape(-1)
    col = cc.reshape(-1)
    n_freq = RET_DK // 4
    inv = ROPE_BASE ** (-jnp.arange(n_freq, dtype=jnp.float32) / n_freq)
    ang = jnp.concatenate([row[:, None] * inv, col[:, None] * inv], axis=-1)
    return jnp.cos(ang), jnp.sin(ang)


def apply_rope(x, cos, sin):
    xp = x.reshape(x.shape[:-1] + (-1, 2))
    x0, x1 = xp[..., 0], xp[..., 1]
    c = cos[None, :, None, :]
    s = sin[None, :, None, :]
    return jnp.stack([x0 * c - x1 * s, x0 * s + x1 * c], axis=-1).reshape(x.shape)


def retention_chunkwise(q, k, v, log_g, s0):
    B, L, H, dk = q.shape
    dv = v.shape[-1]
    C = RET_CHUNK
    n = L // C
    qc = q.reshape(B, n, C, H, dk)
    kc = k.reshape(B, n, C, H, dk)
    vc = v.reshape(B, n, C, H, dv)
    idx = jnp.arange(C, dtype=jnp.float32)
    diff = idx[:, None] - idx[None, :]
    dmask = jnp.where(diff[None] >= 0,
                      jnp.exp(jnp.maximum(diff, 0.0)[None] * log_g[:, None, None]), 0.0)
    scores = jnp.einsum('bnihd,bnjhd->bnhij', qc, kc) * dmask
    intra = jnp.einsum('bnhij,bnjhe->bnihe', scores, vc)
    k_dec = kc * jnp.exp((C - 1 - idx)[:, None] * log_g[None, :])[:, :, None]
    chunk_kv = jnp.einsum('bnjhd,bnjhe->nbhde', k_dec, vc)
    g_chunk = jnp.exp(C * log_g)[None, :, None, None]

    def step(s, kv):
        return g_chunk * s + kv, s

    _, s_before = lax.scan(step, s0, chunk_kv)
    q_dec = qc * jnp.exp((idx + 1)[:, None] * log_g[None, :])[:, :, None]
    cross = jnp.einsum('bnihd,nbhde->bnihe', q_dec, s_before)
    return (intra + cross).reshape(B, L, H, dv)


def retention_final_states(k, v, log_g):
    L = k.shape[1]
    t = jnp.arange(L, dtype=jnp.float32)
    wf = jnp.exp((L - 1 - t)[:, None] * log_g[0][None, :])
    wb = jnp.exp(t[:, None] * log_g[1][None, :])
    s_f = jnp.einsum('blhd,lh,blhe->bhde', k, wf, v)
    s_b = jnp.einsum('blhd,lh,blhe->bhde', k, wb, v)
    return s_f, s_b


def bidir_retention(q, k, v, log_g, s0_f, s0_b):
    o_f = retention_chunkwise(q, k, v, log_g[0], s0_f)
    fl = lambda a: jnp.flip(a, axis=1)
    o_b = fl(retention_chunkwise(fl(q), fl(k), fl(v), log_g[1], s0_b))
    return o_f + o_b


def retention_branch(q, k, v, g, s0_f, s0_b, lp):
    o = bidir_retention(q, k, v, lp['log_g'], s0_f, s0_b)
    mu = jnp.mean(o, axis=-1, keepdims=True)
    var = jnp.mean(jnp.square(o - mu), axis=-1, keepdims=True)
    o = (o - mu) * lax.rsqrt(var + EPS)
    o = o.reshape(o.shape[0], o.shape[1], -1) * lp['ret_gn_g'].astype(jnp.float32)
    return matmul(o.astype(g.dtype) * jax.nn.silu(g), lp['w_br_ret'])


def conformer_branch(pc, lp):
    a, b = jnp.split(pc, 2, axis=-1)
    h = depthwise_conv(a * jax.nn.sigmoid(b), lp['conf_dw_w'], lp['conf_dw_b'])
    hf = h.astype(jnp.float32)
    mu = jnp.mean(hf, axis=-1, keepdims=True)
    var = jnp.mean(jnp.square(hf - mu), axis=-1, keepdims=True)
    hn = (hf - mu) * lax.rsqrt(var + EPS) * lp['conf_ln_g'].astype(jnp.float32) + lp['conf_ln_b'].astype(jnp.float32)
    return matmul(jax.nn.silu(hn).astype(pc.dtype), lp['w_br_conf'])


def hyena_filters(L, lp):
    f32 = jnp.float32
    t = jnp.arange(L, dtype=f32)
    t_norm = t / max(L - 1, 1)
    w_ang = 2.0 * math.pi * t / L
    bands = jnp.linspace(1e-4, HY_BANDS - 1, HY_BANDS, dtype=f32)
    feat = jnp.concatenate([t_norm[:, None], jnp.cos(w_ang[:, None] * bands),
                            -jnp.sin(w_ang[:, None] * bands)], axis=-1)
    freq = lp['hy_freq'].astype(f32)
    h = jnp.sin(freq * (feat @ lp['hy_w1'].astype(f32) + lp['hy_b1'].astype(f32)))
    h = jnp.sin(freq * (h @ lp['hy_w2'].astype(f32) + lp['hy_b2'].astype(f32)))
    h = (h @ lp['hy_w3'].astype(f32) + lp['hy_b3'].astype(f32)).reshape(L, HY_ORDER, 2, HY_WIDTH)
    deltas = jnp.abs(jnp.linspace(math.log(HY_DECAY_TARGET) / HY_DECAY_LONG_PCT,
                                  math.log(HY_DECAY_TARGET) / HY_DECAY_SHORT_PCT, HY_WIDTH, dtype=f32))
    window = jnp.exp(-t_norm[:, None] * deltas[None, :])
    h = h * window[:, None, None, :]
    h_fwd = h[:, :, 0]
    h_bwd = h[:, :, 1]
    taps = jnp.concatenate([h_fwd, jnp.zeros_like(h_fwd[:1]), jnp.flip(h_bwd[1:], axis=0)], axis=0)
    taps = taps * lax.rsqrt(jnp.sum(taps * taps, axis=0, keepdims=True) + EPS)
    return jnp.fft.rfft(taps, axis=0)


def hyena_branch(ph, filt_fft, lp):
    L = ph.shape[1]
    ph = depthwise_conv(ph, lp['hy_short_w'], lp['hy_short_b'])
    v, x1, x2 = jnp.split(ph, 3, axis=-1)
    z = v.astype(jnp.float32)
    bias = lp['hy_bias'].astype(jnp.float32)
    for o, gate in enumerate((x1, x2)):
        zf = jnp.fft.rfft(z, n=2 * L, axis=1)
        y = jnp.fft.irfft(zf * filt_fft[None, :, o, :], n=2 * L, axis=1)[:, :L]
        z = gate.astype(jnp.float32) * (y + z * bias[o])
    return matmul(z.astype(ph.dtype), lp['w_br_hy'])


def mixer_block(p, s0_f, s0_b, rope, filt_fft, lp):
    q = split_heads(p[..., OFF_Q:OFF_K], RET_DK)
    k = split_heads(p[..., OFF_K:OFF_V], RET_DK) * RET_DK ** -0.5
    v = split_heads(p[..., OFF_V:OFF_G], RET_DV)
    if rope is not None:
        q = apply_rope(q, rope[0], rope[1])
        k = apply_rope(k, rope[0], rope[1])
    b_ret = retention_branch(q, k, v, p[..., OFF_G:OFF_CONF], s0_f, s0_b, lp)
    b_conf = conformer_branch(p[..., OFF_CONF:OFF_HY], lp)
    b_hy = hyena_branch(p[..., OFF_HY:OFF_GATE], filt_fft, lp)
    g_ret, g_conf, g_hy = jnp.split(jax.nn.sigmoid(p[..., OFF_GATE:]), N_BRANCH, axis=-1)
    return matmul(g_ret * b_ret + g_conf * b_conf + g_hy * b_hy, lp['w_out'])


def peer_ffn(h, w_q, sub_keys, u_tab, v_tab):
    B, L, D = h.shape
    T = B * L
    ht = h.reshape(T, D)
    q = matmul(ht, w_q).astype(jnp.float32).reshape(T, PEER_HEADS, 2, PEER_DKEY // 2)
    s = jnp.einsum('thpd,hpkd->thpk', q, sub_keys.astype(jnp.float32))
    s_top, i_top = lax.top_k(s, PEER_TOPK)
    cand_s = s_top[:, :, 0, :, None] + s_top[:, :, 1, None, :]
    cand_i = i_top[:, :, 0, :, None] * PEER_NKEYS + i_top[:, :, 1, None, :]
    best_s, best_j = lax.top_k(cand_s.reshape(T, PEER_HEADS, -1), PEER_TOPK)
    experts = jnp.take_along_axis(cand_i.reshape(T, PEER_HEADS, -1), best_j, axis=-1)
    gates = jax.nn.softmax(best_s, axis=-1)
    n_blk = T // PEER_TOKEN_BLOCK

    def expert_block(args):
        hb, eb, gb = args
        a = jnp.einsum('tkd,td->tk', u_tab[eb], hb)
        w = (jax.nn.gelu(a.astype(jnp.float32)) * gb).astype(hb.dtype)
        return jnp.einsum('tk,tkd->td', w, v_tab[eb])

    out = lax.map(expert_block, (ht.reshape(n_blk, PEER_TOKEN_BLOCK, D),
                                 experts.reshape(n_blk, PEER_TOKEN_BLOCK, -1),
                                 gates.reshape(n_blk, PEER_TOKEN_BLOCK, -1)))
    return out.reshape(B, L, D)


def kernel(x, c, ctx, c_ctx, ada_w, ada_b, norm1_g, norm2_g, w_in, ret_decay_logit, ret_gn_g, w_br_ret, conf_dw_w, conf_dw_b, conf_ln_g, conf_ln_b, w_br_conf, hy_short_w, hy_short_b, hy_w1, hy_b1, hy_freq, hy_w2, hy_b2, hy_w3, hy_b3, hy_bias, w_br_hy, w_out, peer_wq, peer_keys, peer_u, peer_v, final_g):
    depth = ada_w.shape[0]
    L = x.shape[1]
    Lc = ctx.shape[1]
    rope = grid_rope_angles(L // GRID_W)
    for l in range(depth):
        last = l == depth - 1
        lp = {
            'log_g': jax.nn.log_sigmoid(ret_decay_logit[l].astype(jnp.float32)),
            'ret_gn_g': ret_gn_g[l], 'w_br_ret': w_br_ret[l],
            'conf_dw_w': conf_dw_w[l], 'conf_dw_b': conf_dw_b[l],
            'conf_ln_g': conf_ln_g[l], 'conf_ln_b': conf_ln_b[l], 'w_br_conf': w_br_conf[l],
            'hy_short_w': hy_short_w[l], 'hy_short_b': hy_short_b[l],
            'hy_w1': hy_w1[l], 'hy_b1': hy_b1[l], 'hy_freq': hy_freq[l],
            'hy_w2': hy_w2[l], 'hy_b2': hy_b2[l], 'hy_w3': hy_w3[l], 'hy_b3': hy_b3[l],
            'hy_bias': hy_bias[l], 'w_br_hy': w_br_hy[l], 'w_out': w_out[l],
        }
        mod = adaln(c, ada_w[l], ada_b[l])
        mod_c = adaln(c_ctx[None, :], ada_w[l], ada_b[l])

        u_c = modulate(rms_norm(ctx, norm1_g[l]), mod_c[0], mod_c[1])
        if last:
            pk_c = matmul(u_c, w_in[l][:, OFF_K:OFF_V])
            pv_c = matmul(u_c, w_in[l][:, OFF_V:OFF_G])
        else:
            p_c = matmul(u_c, w_in[l])
            pk_c = p_c[..., OFF_K:OFF_V]
            pv_c = p_c[..., OFF_V:OFF_G]
        s_f, s_b = retention_final_states(split_heads(pk_c, RET_DK) * RET_DK ** -0.5,
                                          split_heads(pv_c, RET_DV), lp['log_g'])

        u = modulate(rms_norm(x, norm1_g[l]), mod[0], mod[1])
        p = matmul(u, w_in[l])
        x = x + mod[2] * mixer_block(p, s_f, s_b, rope, hyena_filters(L, lp), lp)

        x = x + mod[5] * peer_ffn(modulate(rms_norm(x, norm2_g[l]), mod[3], mod[4]),
                                  peer_wq[l], peer_keys[l], peer_u[l], peer_v[l])

        if not last:
            zero = jnp.zeros_like(s_f)
            ctx = ctx + mod_c[2] * mixer_block(p_c, zero, zero, None, hyena_filters(Lc, lp), lp)
            ctx = ctx + mod_c[5] * peer_ffn(modulate(rms_norm(ctx, norm2_g[l]), mod_c[3], mod_c[4]),
                                            peer_wq[l], peer_keys[l], peer_u[l], peer_v[l])
    return rms_norm(x, final_g)
```

```python
import functools
import math

import jax
import jax.numpy as jnp
import numpy as np
from jax import lax
from jax.experimental import pallas as pl
from jax.experimental.pallas import tpu as pltpu

D_MODEL = 1024
GRID_W = 64
EPS = 1e-6

RET_HEADS = 8
RET_DV = D_MODEL // RET_HEADS
RET_DK = RET_DV // 2
RET_CHUNK = 128
ROPE_BASE = 10000.0

CONF_WIDTH = D_MODEL // 2
CONF_TAPS = 31

HY_WIDTH = D_MODEL // 2
HY_ORDER = 2
HY_SHORT_TAPS = 3
HY_BANDS = 16
HY_EMB = 2 * HY_BANDS + 1
HY_HIDDEN = 64
HY_DECAY_SHORT_PCT = 0.3
HY_DECAY_LONG_PCT = 1.5
HY_DECAY_TARGET = 1e-2

PEER_HEADS = 8
PEER_NKEYS = 128
PEER_DKEY = D_MODEL // PEER_HEADS
PEER_TOPK = 16
PEER_TOKEN_BLOCK = 128

N_BRANCH = 3
OFF_Q = 0
OFF_K = OFF_Q + RET_HEADS * RET_DK
OFF_V = OFF_K + RET_HEADS * RET_DK
OFF_G = OFF_V + RET_HEADS * RET_DV
OFF_CONF = OFF_G + RET_HEADS * RET_DV
OFF_HY = OFF_CONF + 2 * CONF_WIDTH
OFF_GATE = OFF_HY + (HY_ORDER + 1) * HY_WIDTH
IN_COLS = OFF_GATE + N_BRANCH * D_MODEL


def _matmul_body(x_ref, w_ref, o_ref):
    o_ref[...] = jnp.dot(x_ref[...].astype(jnp.bfloat16), w_ref[...].astype(jnp.bfloat16),
                         preferred_element_type=jnp.float32)


def _pick_tile(n, pref):
    t = min(n, pref)
    while n % t:
        t //= 2
    return t


def matmul(x, w):
    lead = x.shape[:-1]
    x2 = x.reshape(-1, x.shape[-1])
    M, K = x2.shape
    N = w.shape[1]
    tm = _pick_tile(M, 512)
    tn = _pick_tile(N, 512)
    out = pl.pallas_call(
        _matmul_body,
        grid=(M // tm, N // tn),
        in_specs=[pl.BlockSpec((tm, K), lambda i, j: (i, 0)),
                  pl.BlockSpec((K, tn), lambda i, j: (0, j))],
        out_specs=pl.BlockSpec((tm, tn), lambda i, j: (i, j)),
        out_shape=jax.ShapeDtypeStruct((M, N), jnp.float32),
        compiler_params=pltpu.CompilerParams(dimension_semantics=("parallel", "parallel")),
        name="matmul",
    )(x2, w)
    return out.reshape(lead + (N,))


def rms_norm(x, gain):
    xf = x.astype(jnp.float32)
    y = xf * lax.rsqrt(jnp.mean(xf * xf, axis=-1, keepdims=True) + EPS)
    return (y * gain.astype(jnp.float32)).astype(x.dtype)


def modulate(h, shift, scale):
    return h * (1 + scale) + shift


def adaln(cvec, w, b):
    m = (jax.nn.silu(cvec) @ w + b)[:, None, :]
    return jnp.split(m, 6, axis=-1)


def depthwise_conv(x, w, b):
    taps = w.shape[0]
    left = (taps - 1) // 2
    y = lax.conv_general_dilated(
        x, w[:, None, :].astype(x.dtype), window_strides=(1,),
        padding=[(left, taps - 1 - left)],
        dimension_numbers=('NWC', 'WIO', 'NWC'),
        feature_group_count=x.shape[-1])
    return y + b.astype(x.dtype)


def split_heads(a, d):
    return a.astype(jnp.float32).reshape(a.shape[0], a.shape[1], -1, d)


def grid_rope_angles(rows):
    rr, cc = jnp.meshgrid(jnp.arange(rows, dtype=jnp.float32),
                          jnp.arange(GRID_W, dtype=jnp.float32), indexing='ij')
    row = rr.reshape(-1)
    col = cc.reshape(-1)
    n_freq = RET_DK // 4
    inv = ROPE_BASE ** (-jnp.arange(n_freq, dtype=jnp.float32) / n_freq)
    ang = jnp.concatenate([row[:, None] * inv, col[:, None] * inv], axis=-1)
    return jnp.cos(ang), jnp.sin(ang)


def apply_rope(x, cos, sin):
    xp = x.reshape(x.shape[:-1] + (-1, 2))
    x0, x1 = xp[..., 0], xp[..., 1]
    c = cos[None, :, None, :]
    s = sin[None, :, None, :]
    return jnp.stack([x0 * c - x1 * s, x0 * s + x1 * c], axis=-1).reshape(x.shape)


def retention_chunkwise(q, k, v, log_g, s0):
    B, L, H, dk = q.shape
    dv = v.shape[-1]
    C = RET_CHUNK
    n = L // C
    qc = q.reshape(B, n, C, H, dk)
    kc = k.reshape(B, n, C, H, dk)
    vc = v.reshape(B, n, C, H, dv)
    idx = jnp.arange(C, dtype=jnp.float32)
    diff = idx[:, None] - idx[None, :]
    dmask = jnp.where(diff[None] >= 0,
                      jnp.exp(jnp.maximum(diff, 0.0)[None] * log_g[:, None, None]), 0.0)
    scores = jnp.einsum('bnihd,bnjhd->bnhij', qc, kc) * dmask
    intra = jnp.einsum('bnhij,bnjhe->bnihe', scores, vc)
    k_dec = kc * jnp.exp((C - 1 - idx)[:, None] * log_g[None, :])[:, :, None]
    chunk_kv = jnp.einsum('bnjhd,bnjhe->nbhde', k_dec, vc)
    g_chunk = jnp.exp(C * log_g)[None, :, None, None]

    def step(s, kv):
        return g_chunk * s + kv, s

    _, s_before = lax.scan(step, s0, chunk_kv)
    q_dec = qc * jnp.exp((idx + 1)[:, None] * log_g[None, :])[:, :, None]
    cross = jnp.einsum('bnihd,nbhde->bnihe', q_dec, s_before)
    return (intra + cross).reshape(B, L, H, dv)


def retention_final_states(k, v, log_g):
    L = k.shape[1]
    t = jnp.arange(L, dtype=jnp.float32)
    wf = jnp.exp((L - 1 - t)[:, None] * log_g[0][None, :])
    wb = jnp.exp(t[:, None] * log_g[1][None, :])
    s_f = jnp.einsum('blhd,lh,blhe->bhde', k, wf, v)
    s_b = jnp.einsum('blhd,lh,blhe->bhde', k, wb, v)
    return s_f, s_b


def bidir_retention(q, k, v, log_g, s0_f, s0_b):
    o_f = retention_chunkwise(q, k, v, log_g[0], s0_f)
    fl = lambda a: jnp.flip(a, axis=1)
    o_b = fl(retention_chunkwise(fl(q), fl(k), fl(v), log_g[1], s0_b))
    return o_f + o_b


def retention_branch(q, k, v, g, s0_f, s0_b, lp):
    o = bidir_retention(q, k, v, lp['log_g'], s0_f, s0_b)
    mu = jnp.mean(o, axis=-1, keepdims=True)
    var = jnp.mean(jnp.square(o - mu), axis=-1, keepdims=True)
    o = (o - mu) * lax.rsqrt(var + EPS)
    o = o.reshape(o.shape[0], o.shape[1], -1) * lp['ret_gn_g'].astype(jnp.float32)
    return matmul(o.astype(g.dtype) * jax.nn.silu(g), lp['w_br_ret'])


def conformer_branch(pc, lp):
    a, b = jnp.split(pc, 2, axis=-1)
    h = depthwise_conv(a * jax.nn.sigmoid(b), lp['conf_dw_w'], lp['conf_dw_b'])
    hf = h.astype(jnp.float32)
    mu = jnp.mean(hf, axis=-1, keepdims=True)
    var = jnp.mean(jnp.square(hf - mu), axis=-1, keepdims=True)
    hn = (hf - mu) * lax.rsqrt(var + EPS) * lp['conf_ln_g'].astype(jnp.float32) + lp['conf_ln_b'].astype(jnp.float32)
    return matmul(jax.nn.silu(hn).astype(pc.dtype), lp['w_br_conf'])


def hyena_filters(L, lp):
    f32 = jnp.float32
    t = jnp.arange(L, dtype=f32)
    t_norm = t / max(L - 1, 1)
    w_ang = 2.0 * math.pi * t / L
    bands = jnp.linspace(1e-4, HY_BANDS - 1, HY_BANDS, dtype=f32)
    feat = jnp.concatenate([t_norm[:, None], jnp.cos(w_ang[:, None] * bands),
                            -jnp.sin(w_ang[:, None] * bands)], axis=-1)
    freq = lp['hy_freq'].astype(f32)
    h = jnp.sin(freq * (feat @ lp['hy_w1'].astype(f32) + lp['hy_b1'].astype(f32)))
    h = jnp.sin(freq * (h @ lp['hy_w2'].astype(f32) + lp['hy_b2'].astype(f32)))
    h = (h @ lp['hy_w3'].astype(f32) + lp['hy_b3'].astype(f32)).reshape(L, HY_ORDER, 2, HY_WIDTH)
    deltas = jnp.abs(jnp.linspace(math.log(HY_DECAY_TARGET) / HY_DECAY_LONG_PCT,
                                  math.log(HY_DECAY_TARGET) / HY_DECAY_SHORT_PCT, HY_WIDTH, dtype=f32))
    window = jnp.exp(-t_norm[:, None] * deltas[None, :])
    h = h * window[:, None, None, :]
    h_fwd = h[:, :, 0]
    h_bwd = h[:, :, 1]
    taps = jnp.concatenate([h_fwd, jnp.zeros_like(h_fwd[:1]), jnp.flip(h_bwd[1:], axis=0)], axis=0)
    taps = taps * lax.rsqrt(jnp.sum(taps * taps, axis=0, keepdims=True) + EPS)
    return jnp.fft.rfft(taps, axis=0)


def hyena_branch(ph, filt_fft, lp):
    L = ph.shape[1]
    ph = depthwise_conv(ph, lp['hy_short_w'], lp['hy_short_b'])
    v, x1, x2 = jnp.split(ph, 3, axis=-1)
    z = v.astype(jnp.float32)
    bias = lp['hy_bias'].astype(jnp.float32)
    for o, gate in enumerate((x1, x2)):
        zf = jnp.fft.rfft(z, n=2 * L, axis=1)
        y = jnp.fft.irfft(zf * filt_fft[None, :, o, :], n=2 * L, axis=1)[:, :L]
        z = gate.astype(jnp.float32) * (y + z * bias[o])
    return matmul(z.astype(ph.dtype), lp['w_br_hy'])


def mixer_block(p, s0_f, s0_b, rope, filt_fft, lp):
    q = split_heads(p[..., OFF_Q:OFF_K], RET_DK)
    k = split_heads(p[..., OFF_K:OFF_V], RET_DK) * RET_DK ** -0.5
    v = split_heads(p[..., OFF_V:OFF_G], RET_DV)
    if rope is not None:
        q = apply_rope(q, rope[0], rope[1])
        k = apply_rope(k, rope[0], rope[1])
    b_ret = retention_branch(q, k, v, p[..., OFF_G:OFF_CONF], s0_f, s0_b, lp)
    b_conf = conformer_branch(p[..., OFF_CONF:OFF_HY], lp)
    b_hy = hyena_branch(p[..., OFF_HY:OFF_GATE], filt_fft, lp)
    g_ret, g_conf, g_hy = jnp.split(jax.nn.sigmoid(p[..., OFF_GATE:]), N_BRANCH, axis=-1)
    return matmul(g_ret * b_ret + g_conf * b_conf + g_hy * b_hy, lp['w_out'])


LOG2E = 1.4426950408889634
NEG_BIG = -1e30
PEER_ROUTE_TM = 512
PEER_TM = 512
PEER_TE = 1024
PEER_ROWS = PEER_TE // PEER_NKEYS
LANES = 128


def _batcher_pairs(lo, hi):
    def merge(lo, hi, r):
        step = r * 2
        if step < hi - lo:
            yield from merge(lo, hi, step)
            yield from merge(lo + r, hi, step)
            yield from [(i, i + r) for i in range(lo + r, hi - r, step)]
        else:
            yield (lo, lo + r)
    if hi - lo >= 1:
        mid = lo + (hi - lo) // 2
        yield from _batcher_pairs(lo, mid)
        yield from _batcher_pairs(mid + 1, hi)
        yield from merge(lo, hi, 1)


SORT16 = tuple(_batcher_pairs(0, PEER_TOPK - 1))


def _cmpx(v, i, j):
    a, b = v[i], v[j]
    v[i] = jnp.maximum(a, b)
    v[j] = jnp.minimum(a, b)


def _bitonic_desc(v):
    for dist in (8, 4, 2, 1):
        for i in range(PEER_TOPK):
            if not i & dist:
                _cmpx(v, i, i + dist)


def _merge_top(cur, other, dropped):
    out = list(cur)
    for r, o in enumerate(other):
        i = PEER_TOPK - 1 - r
        out[i] = jnp.maximum(cur[i], o)
        dropped = jnp.maximum(dropped, jnp.minimum(cur[i], o))
    _bitonic_desc(out)
    return out, dropped


def _top17_of_rows(s):
    v = [s[8 * k:8 * k + 8, :] for k in range(PEER_NKEYS // 8)]
    for i, j in SORT16:
        _cmpx(v, i, j)
    dropped = jnp.full(v[0].shape, NEG_BIG, jnp.float32)
    for shift in (4, 2, 1):
        other = [pltpu.roll(x, shift, 0) for x in v]
        dropped = jnp.maximum(dropped, pltpu.roll(dropped, shift, 0))
        v, dropped = _merge_top(v, other, dropped)
    return v + [dropped]


def _peer_route_body(hT_ref, wqT_ref, keys_ref, s1p_ref, thr_ref, s2_ref):
    K = PEER_TOPK
    half = PEER_DKEY // 2
    qT = jnp.dot(wqT_ref[...], hT_ref[...], preferred_element_type=jnp.float32)
    s1 = jnp.dot(keys_ref[0], qT[:half], preferred_element_type=jnp.float32,
                 precision=lax.Precision.HIGHEST)
    s2 = jnp.dot(keys_ref[1], qT[half:], preferred_element_type=jnp.float32,
                 precision=lax.Precision.HIGHEST)
    t1 = _top17_of_rows(s1)
    t2 = _top17_of_rows(s2)
    best = [t1[0] + t2[b] for b in range(K)]
    dropped = t1[0] + t2[K]
    for a in range(1, 8):
        best, dropped = _merge_top(best, [t1[a] + t2[b] for b in range((K + 1) // (a + 1))], dropped)
    best, dropped = _merge_top(best, [t1[a] + t2[0] for a in range(8, K + 1)], dropped)
    tau = 0.5 * (best[K - 1] + dropped)
    z = jnp.zeros_like(tau)
    for r in range(K):
        z = z + jnp.exp(best[r] - best[0])
    lse = best[0] + jnp.log(z)
    s1p_ref[0] = (s1 - lse[0:1]) * LOG2E
    thr_ref[0] = (tau[0:1] - s1) * LOG2E
    s2_ref[0] = s2 * LOG2E


def peer_route(hT, wqT, keys):
    D, T = hT.shape
    tm = min(PEER_ROUTE_TM, T)
    assert T % tm == 0
    shp = jax.ShapeDtypeStruct((PEER_HEADS, PEER_NKEYS, T), jnp.float32)
    ospec = pl.BlockSpec((1, PEER_NKEYS, tm), lambda t, h: (h, 0, t))
    return pl.pallas_call(
        _peer_route_body,
        grid=(T // tm, PEER_HEADS),
        in_specs=[pl.BlockSpec((D, tm), lambda t, h: (0, t)),
                  pl.BlockSpec((PEER_DKEY, D), lambda t, h: (h, 0)),
                  pl.BlockSpec((2, PEER_NKEYS, PEER_DKEY // 2), lambda t, h: (h, 0, 0))],
        out_specs=[ospec, ospec, ospec],
        out_shape=[shp, shp, shp],
        compiler_params=pltpu.CompilerParams(dimension_semantics=("parallel", "arbitrary")),
        name="peer_route",
    )(hT, wqT, keys.reshape(PEER_HEADS * 2, PEER_NKEYS, PEER_DKEY // 2))


def _gelu_tanh(a):
    inner = a * (0.7978845608028654 + 0.035677408136300125 * (a * a))
    half = 0.5 * a
    return half + half * jnp.tanh(inner)


def _peer_expert_body(u_ref, hT_ref, vT_ref, s1p_ref, thr_ref, s2_ref, o_ref, a_scr, w_scr):
    tm = hT_ref.shape[1]
    a_scr[...] = jnp.dot(u_ref[...], hT_ref[...], preferred_element_type=jnp.float32)

    for r in range(PEER_ROWS):
        for g in range(tm // LANES):
            lanes = pl.ds(g * LANES, LANES)
            s1b = [jnp.broadcast_to(s1p_ref[h, r:r + 1, lanes], (16, LANES))
                   for h in range(PEER_HEADS)]
            thb = [jnp.broadcast_to(thr_ref[h, r:r + 1, lanes], (16, LANES))
                   for h in range(PEER_HEADS)]
            for jb in range(PEER_NKEYS // 16):
                rows = pl.ds(r * PEER_NKEYS + jb * 16, 16)
                gate = jnp.zeros((16, LANES), jnp.float32)
                for h in range(PEER_HEADS):
                    s2 = s2_ref[h, jb * 16:(jb + 1) * 16, lanes]
                    gate = gate + jnp.where(s2 >= thb[h], jnp.exp2(s2 + s1b[h]), 0.0)
                w_scr[rows, lanes] = (_gelu_tanh(a_scr[rows, lanes]) * gate).astype(jnp.bfloat16)
    contrib = jnp.dot(vT_ref[...], w_scr[...], preferred_element_type=jnp.float32)

    @pl.when(pl.program_id(1) == 0)
    def _():
        o_ref[...] = contrib

    @pl.when(pl.program_id(1) != 0)
    def _():
        o_ref[...] += contrib


def peer_experts(hT, u_bf, vT_bf, s1p, thr, s2):
    D, T = hT.shape
    E = u_bf.shape[0]
    tm = min(PEER_TM, T)
    te = PEER_TE
    assert T % tm == 0 and E % te == 0
    return pl.pallas_call(
        _peer_expert_body,
        grid=(T // tm, E // te),
        in_specs=[pl.BlockSpec((te, D), lambda t, e: (e, 0)),
                  pl.BlockSpec((D, tm), lambda t, e: (0, t)),
                  pl.BlockSpec((D, te), lambda t, e: (0, e)),
                  pl.BlockSpec((PEER_HEADS, PEER_ROWS, tm), lambda t, e: (0, e, t)),
                  pl.BlockSpec((PEER_HEADS, PEER_ROWS, tm), lambda t, e: (0, e, t)),
                  pl.BlockSpec((PEER_HEADS, PEER_NKEYS, tm), lambda t, e: (0, 0, t))],
        out_specs=pl.BlockSpec((D, tm), lambda t, e: (0, t)),
        out_shape=jax.ShapeDtypeStruct((D, T), jnp.float32),
        scratch_shapes=[pltpu.VMEM((te, tm), jnp.float32),
                        pltpu.VMEM((te, tm), jnp.bfloat16)],
        compiler_params=pltpu.CompilerParams(dimension_semantics=("parallel", "arbitrary"),
                                             vmem_limit_bytes=48 * 1024 * 1024),
        name="peer_experts",
    )(u_bf, hT, vT_bf, s1p, thr, s2)


def peer_ffn(h, w_q, sub_keys, u_bf, vT_bf):
    B, L, D = h.shape
    hT = h.reshape(B * L, D).T.astype(jnp.bfloat16)
    s1p, thr, s2 = peer_route(hT, w_q.T.astype(jnp.bfloat16), sub_keys.astype(jnp.float32))
    outT = peer_experts(hT, u_bf, vT_bf, s1p, thr, s2)
    return outT.T.reshape(B, L, D)


def kernel(x, c, ctx, c_ctx, ada_w, ada_b, norm1_g, norm2_g, w_in, ret_decay_logit, ret_gn_g, w_br_ret, conf_dw_w, conf_dw_b, conf_ln_g, conf_ln_b, w_br_conf, hy_short_w, hy_short_b, hy_w1, hy_b1, hy_freq, hy_w2, hy_b2, hy_w3, hy_b3, hy_bias, w_br_hy, w_out, peer_wq, peer_keys, peer_u, peer_v, final_g):
    depth = ada_w.shape[0]
    L = x.shape[1]
    Lc = ctx.shape[1]
    rope = grid_rope_angles(L // GRID_W)
    for l in range(depth):
        last = l == depth - 1
        lp = {
            'log_g': jax.nn.log_sigmoid(ret_decay_logit[l].astype(jnp.float32)),
            'ret_gn_g': ret_gn_g[l], 'w_br_ret': w_br_ret[l],
            'conf_dw_w': conf_dw_w[l], 'conf_dw_b': conf_dw_b[l],
            'conf_ln_g': conf_ln_g[l], 'conf_ln_b': conf_ln_b[l], 'w_br_conf': w_br_conf[l],
            'hy_short_w': hy_short_w[l], 'hy_short_b': hy_short_b[l],
            'hy_w1': hy_w1[l], 'hy_b1': hy_b1[l], 'hy_freq': hy_freq[l],
            'hy_w2': hy_w2[l], 'hy_b2': hy_b2[l], 'hy_w3': hy_w3[l], 'hy_b3': hy_b3[l],
            'hy_bias': hy_bias[l], 'w_br_hy': w_br_hy[l], 'w_out': w_out[l],
        }
        u_bf = peer_u[l].astype(jnp.bfloat16)
        vT_bf = peer_v[l].T.astype(jnp.bfloat16)
        mod = adaln(c, ada_w[l], ada_b[l])
        mod_c = adaln(c_ctx[None, :], ada_w[l], ada_b[l])

        u_c = modulate(rms_norm(ctx, norm1_g[l]), mod_c[0], mod_c[1])
        if last:
            pk_c = matmul(u_c, w_in[l][:, OFF_K:OFF_V])
            pv_c = matmul(u_c, w_in[l][:, OFF_V:OFF_G])
        else:
            p_c = matmul(u_c, w_in[l])
            pk_c = p_c[..., OFF_K:OFF_V]
            pv_c = p_c[..., OFF_V:OFF_G]
        s_f, s_b = retention_final_states(split_heads(pk_c, RET_DK) * RET_DK ** -0.5,
                                          split_heads(pv_c, RET_DV), lp['log_g'])

        u = modulate(rms_norm(x, norm1_g[l]), mod[0], mod[1])
        p = matmul(u, w_in[l])
        x = x + mod[2] * mixer_block(p, s_f, s_b, rope, hyena_filters(L, lp), lp)

        x = x + mod[5] * peer_ffn(modulate(rms_norm(x, norm2_g[l]), mod[3], mod[4]),
                                  peer_wq[l], peer_keys[l], u_bf, vT_bf)

        if not last:
            zero = jnp.zeros_like(s_f)
            ctx = ctx + mod_c[2] * mixer_block(p_c, zero, zero, None, hyena_filters(Lc, lp), lp)
            ctx = ctx + mod_c[5] * peer_ffn(modulate(rms_norm(ctx, norm2_g[l]), mod_c[3], mod_c[4]),
                                            peer_wq[l], peer_keys[l], u_bf, vT_bf)
    return rms_norm(x, final_g)
```

```python
import functools
import math

import jax
import jax.numpy as jnp
import numpy as np
from jax import lax
from jax.experimental import pallas as pl
from jax.experimental.pallas import tpu as pltpu

D_MODEL = 1024
GRID_W = 64
EPS = 1e-6

RET_HEADS = 8
RET_DV = D_MODEL // RET_HEADS
RET_DK = RET_DV // 2
RET_CHUNK = 128
ROPE_BASE = 10000.0

CONF_WIDTH = D_MODEL // 2
CONF_TAPS = 31

HY_WIDTH = D_MODEL // 2
HY_ORDER = 2
HY_SHORT_TAPS = 3
HY_BANDS = 16
HY_EMB = 2 * HY_BANDS + 1
HY_HIDDEN = 64
HY_DECAY_SHORT_PCT = 0.3
HY_DECAY_LONG_PCT = 1.5
HY_DECAY_TARGET = 1e-2

PEER_HEADS = 8
PEER_NKEYS = 128
PEER_DKEY = D_MODEL // PEER_HEADS
PEER_TOPK = 16
PEER_TOKEN_BLOCK = 128

N_BRANCH = 3
OFF_Q = 0
OFF_K = OFF_Q + RET_HEADS * RET_DK
OFF_V = OFF_K + RET_HEADS * RET_DK
OFF_G = OFF_V + RET_HEADS * RET_DV
OFF_CONF = OFF_G + RET_HEADS * RET_DV
OFF_HY = OFF_CONF + 2 * CONF_WIDTH
OFF_GATE = OFF_HY + (HY_ORDER + 1) * HY_WIDTH
IN_COLS = OFF_GATE + N_BRANCH * D_MODEL


def _matmul_body(x_ref, w_ref, o_ref):
    o_ref[...] = jnp.dot(x_ref[...].astype(jnp.bfloat16), w_ref[...].astype(jnp.bfloat16),
                         preferred_element_type=jnp.float32)


def _pick_tile(n, pref):
    t = min(n, pref)
    while n % t:
        t //= 2
    return t


def matmul(x, w):
    lead = x.shape[:-1]
    x2 = x.reshape(-1, x.shape[-1])
    M, K = x2.shape
    N = w.shape[1]
    tm = _pick_tile(M, 512)
    tn = _pick_tile(N, 512)
    out = pl.pallas_call(
        _matmul_body,
        grid=(M // tm, N // tn),
        in_specs=[pl.BlockSpec((tm, K), lambda i, j: (i, 0)),
                  pl.BlockSpec((K, tn), lambda i, j: (0, j))],
        out_specs=pl.BlockSpec((tm, tn), lambda i, j: (i, j)),
        out_shape=jax.ShapeDtypeStruct((M, N), jnp.float32),
        compiler_params=pltpu.CompilerParams(dimension_semantics=("parallel", "parallel")),
        name="matmul",
    )(x2, w)
    return out.reshape(lead + (N,))


def rms_norm(x, gain):
    xf = x.astype(jnp.float32)
    y = xf * lax.rsqrt(jnp.mean(xf * xf, axis=-1, keepdims=True) + EPS)
    return (y * gain.astype(jnp.float32)).astype(x.dtype)


def modulate(h, shift, scale):
    return h * (1 + scale) + shift


def adaln(cvec, w, b):
    m = (jax.nn.silu(cvec) @ w + b)[:, None, :]
    return jnp.split(m, 6, axis=-1)


def depthwise_conv(x, w, b):
    taps = w.shape[0]
    left = (taps - 1) // 2
    y = lax.conv_general_dilated(
        x, w[:, None, :].astype(x.dtype), window_strides=(1,),
        padding=[(left, taps - 1 - left)],
        dimension_numbers=('NWC', 'WIO', 'NWC'),
        feature_group_count=x.shape[-1])
    return y + b.astype(x.dtype)


def split_heads(a, d):
    return a.astype(jnp.float32).reshape(a.shape[0], a.shape[1], -1, d)


def grid_rope_angles(rows):
    rr, cc = jnp.meshgrid(jnp.arange(rows, dtype=jnp.float32),
                          jnp.arange(GRID_W, dtype=jnp.float32), indexing='ij')
    row = rr.reshape(-1)
    col = cc.reshape(-1)
    n_freq = RET_DK // 4
    inv = ROPE_BASE ** (-jnp.arange(n_freq, dtype=jnp.float32) / n_freq)
    ang = jnp.concatenate([row[:, None] * inv, col[:, None] * inv], axis=-1)
    return jnp.cos(ang), jnp.sin(ang)


def apply_rope(x, cos, sin):
    xp = x.reshape(x.shape[:-1] + (-1, 2))
    x0, x1 = xp[..., 0], xp[..., 1]
    c = cos[None, :, None, :]
    s = sin[None, :, None, :]
    return jnp.stack([x0 * c - x1 * s, x0 * s + x1 * c], axis=-1).reshape(x.shape)


def conformer_branch(pc, lp):
    a, b = jnp.split(pc, 2, axis=-1)
    h = depthwise_conv(a * jax.nn.sigmoid(b), lp['conf_dw_w'], lp['conf_dw_b'])
    hf = h.astype(jnp.float32)
    mu = jnp.mean(hf, axis=-1, keepdims=True)
    var = jnp.mean(jnp.square(hf - mu), axis=-1, keepdims=True)
    hn = (hf - mu) * lax.rsqrt(var + EPS) * lp['conf_ln_g'].astype(jnp.float32) + lp['conf_ln_b'].astype(jnp.float32)
    return matmul(jax.nn.silu(hn).astype(pc.dtype), lp['w_br_conf'])


def hyena_filters(L, lp):
    f32 = jnp.float32
    t = jnp.arange(L, dtype=f32)
    t_norm = t / max(L - 1, 1)
    w_ang = 2.0 * math.pi * t / L
    bands = jnp.linspace(1e-4, HY_BANDS - 1, HY_BANDS, dtype=f32)
    feat = jnp.concatenate([t_norm[:, None], jnp.cos(w_ang[:, None] * bands),
                            -jnp.sin(w_ang[:, None] * bands)], axis=-1)
    freq = lp['hy_freq'].astype(f32)
    h = jnp.sin(freq * (feat @ lp['hy_w1'].astype(f32) + lp['hy_b1'].astype(f32)))
    h = jnp.sin(freq * (h @ lp['hy_w2'].astype(f32) + lp['hy_b2'].astype(f32)))
    h = (h @ lp['hy_w3'].astype(f32) + lp['hy_b3'].astype(f32)).reshape(L, HY_ORDER, 2, HY_WIDTH)
    deltas = jnp.abs(jnp.linspace(math.log(HY_DECAY_TARGET) / HY_DECAY_LONG_PCT,
                                  math.log(HY_DECAY_TARGET) / HY_DECAY_SHORT_PCT, HY_WIDTH, dtype=f32))
    window = jnp.exp(-t_norm[:, None] * deltas[None, :])
    h = h * window[:, None, None, :]
    h_fwd = h[:, :, 0]
    h_bwd = h[:, :, 1]
    taps = jnp.concatenate([h_fwd, jnp.zeros_like(h_fwd[:1]), jnp.flip(h_bwd[1:], axis=0)], axis=0)
    taps = taps * lax.rsqrt(jnp.sum(taps * taps, axis=0, keepdims=True) + EPS)
    return jnp.fft.rfft(taps, axis=0)


def hyena_branch(ph, filt_fft, lp):
    L = ph.shape[1]
    ph = depthwise_conv(ph, lp['hy_short_w'], lp['hy_short_b'])
    v, x1, x2 = jnp.split(ph, 3, axis=-1)
    z = v.astype(jnp.float32)
    bias = lp['hy_bias'].astype(jnp.float32)
    for o, gate in enumerate((x1, x2)):
        zf = jnp.fft.rfft(z, n=2 * L, axis=1)
        y = jnp.fft.irfft(zf * filt_fft[None, :, o, :], n=2 * L, axis=1)[:, :L]
        z = gate.astype(jnp.float32) * (y + z * bias[o])
    return matmul(z.astype(ph.dtype), lp['w_br_hy'])


def mixer_block(p, ret_pre, lp):
    B, L, _ = p.shape
    b_ret = matmul(ret_pre, lp['w_br_ret']).reshape(B, L, -1)
    b_conf = conformer_branch(p[..., OFF_CONF:OFF_HY], lp)
    if L == FFT_L:
        b_hy = matmul(hyena_long(p[..., OFF_HY:OFF_GATE], lp), lp['w_br_hy']).reshape(B, L, -1)
    else:
        b_hy = hyena_branch(p[..., OFF_HY:OFF_GATE], hyena_filters(L, lp), lp)
    g_ret, g_conf, g_hy = jnp.split(jax.nn.sigmoid(p[..., OFF_GATE:IN_COLS]), N_BRANCH, axis=-1)
    return matmul(g_ret * b_ret + g_conf * b_conf + g_hy * b_hy, lp['w_out'])


BF = jnp.bfloat16
F32 = jnp.float32
RET_TL = 256
VMEM_LIMIT = 48 * 1024 * 1024


def retention_tables(log_g, backward):
    C = RET_CHUNK
    idx = jnp.arange(C, dtype=F32)
    diff = idx[:, None] - idx[None, :]
    if backward:
        diff = -diff
        qe = C - idx
        ke = idx
    else:
        qe = idx + 1
        ke = C - 1 - idx
    dmask = jnp.where(diff[None] >= 0, jnp.exp(jnp.maximum(diff, 0.0)[None] * log_g[:, None, None]), 0.0)
    qdec = jnp.repeat(jnp.exp(qe[:, None] * log_g[None, :]), RET_DK, axis=1)
    kdec = jnp.repeat(jnp.exp(ke[:, None] * log_g[None, :]), RET_DK, axis=1)
    gc = jnp.broadcast_to(jnp.exp(C * log_g)[:, None, None], (RET_HEADS, 2 * RET_DK, RET_DV))
    return dmask, qdec, kdec, gc


def _ret_body(*refs, backward, nchunk, epilogue):
    if epilogue:
        (q_ref, k_ref, qp_ref, kp_ref, v_ref, cos_ref, sin_ref, dmask_ref, qdec_ref, kdec_ref, gc_ref,
         s0_ref, of_ref, g_ref, gn_ref, o_ref, sfin_ref, s_scr) = refs
    else:
        (q_ref, k_ref, qp_ref, kp_ref, v_ref, cos_ref, sin_ref, dmask_ref, qdec_ref, kdec_ref, gc_ref,
         s0_ref, o_ref, sfin_ref, s_scr) = refs
    C = RET_CHUNK

    @pl.when(pl.program_id(1) == 0)
    def _():
        s_scr[...] = s0_ref[0]

    low = lax.broadcasted_iota(jnp.int32, (C, 2 * RET_DK), 1) < RET_DK
    for ci in range(nchunk):
        c = nchunk - 1 - ci if backward else ci
        rows = slice(c * C, (c + 1) * C)
        cosv = cos_ref[rows, :]
        sinv = sin_ref[rows, :]
        q = q_ref[rows, :] * cosv + qp_ref[rows, :] * sinv
        k = (k_ref[rows, :] * cosv + kp_ref[rows, :] * sinv) * (RET_DK ** -0.5)
        qd = q * qdec_ref[...]
        kd = k * kdec_ref[...]
        for h in range(RET_HEADS):
            grp = slice((h // 2) * 128, (h // 2 + 1) * 128)
            keep = low if h % 2 == 0 else jnp.logical_not(low)
            hv = slice(h * RET_DV, (h + 1) * RET_DV)
            qh = q[:, grp].astype(BF)
            kh = jnp.where(keep, k[:, grp], 0.0).astype(BF)
            kdh = jnp.where(keep, kd[:, grp], 0.0).astype(BF)
            vh = v_ref[rows, hv].astype(BF)
            sc = lax.dot_general(qh, kh, (((1,), (1,)), ((), ())), preferred_element_type=F32)
            sc = sc * dmask_ref[h]
            o = jnp.dot(sc.astype(BF), vh, preferred_element_type=F32)
            o = o + jnp.dot(qd[:, grp].astype(BF), s_scr[h].astype(BF), preferred_element_type=F32)
            kv = lax.dot_general(kdh, vh, (((0,), (0,)), ((), ())), preferred_element_type=F32)
            s_scr[h] = gc_ref[h] * s_scr[h] + kv
            if epilogue:
                o = o + of_ref[rows, hv]
                mu = jnp.mean(o, axis=-1, keepdims=True)
                d = o - mu
                var = jnp.mean(d * d, axis=-1, keepdims=True)
                o = d * lax.rsqrt(var + EPS) * gn_ref[:, hv]
                gg = g_ref[rows, hv]
                o = o * (gg * jax.nn.sigmoid(gg))
            o_ref[rows, hv] = o.astype(o_ref.dtype)

    @pl.when(pl.program_id(1) == pl.num_programs(1) - 1)
    def _():
        sfin_ref[0] = s_scr[...]


def retention_pass(p, B, L, cosf, sinf, tables, s0, *, backward, col_q, col_k, col_qp, col_kp, col_v,
                   o_fwd=None, col_g=None, gn_g=None):
    epilogue = o_fwd is not None
    tl = min(RET_TL, L)
    nt = L // tl
    dmask, qdec, kdec, gc = tables
    HK = RET_HEADS * RET_DK
    HV = RET_HEADS * RET_DV

    def pos(b, n):
        return nt - 1 - n if backward else n

    def tok(b, n):
        return b * nt + pos(b, n)

    def colspec(width, col):
        return pl.BlockSpec((tl, width), lambda b, n: (tok(b, n), col))

    const3 = lambda shape: pl.BlockSpec(shape, lambda b, n: (0, 0, 0))
    const2 = lambda shape: pl.BlockSpec(shape, lambda b, n: (0, 0))
    state = pl.BlockSpec((1, RET_HEADS, 2 * RET_DK, RET_DV), lambda b, n: (b, 0, 0, 0))
    in_specs = [colspec(HK, col_q), colspec(HK, col_k), colspec(HK, col_qp), colspec(HK, col_kp),
                colspec(HV, col_v),
                pl.BlockSpec((tl, HK), lambda b, n: (pos(b, n), 0)),
                pl.BlockSpec((tl, HK), lambda b, n: (pos(b, n), 0)),
                const3((RET_HEADS, RET_CHUNK, RET_CHUNK)), const2((RET_CHUNK, HK)), const2((RET_CHUNK, HK)),
                const3((RET_HEADS, 2 * RET_DK, RET_DV)), state]
    args = [p, p, p, p, p, cosf, sinf, dmask, qdec, kdec, gc, s0]
    if epilogue:
        in_specs += [pl.BlockSpec((tl, HV), lambda b, n: (tok(b, n), 0)), colspec(HV, col_g), const2((1, HV))]
        args += [o_fwd, p, gn_g.reshape(1, HV).astype(F32)]
    return pl.pallas_call(
        functools.partial(_ret_body, backward=backward, nchunk=tl // RET_CHUNK, epilogue=epilogue),
        grid=(B, nt),
        in_specs=in_specs,
        out_specs=[pl.BlockSpec((tl, HV), lambda b, n: (tok(b, n), 0)), state],
        out_shape=[jax.ShapeDtypeStruct((B * L, HV), BF if epilogue else F32),
                   jax.ShapeDtypeStruct((B, RET_HEADS, 2 * RET_DK, RET_DV), F32)],
        scratch_shapes=[pltpu.VMEM((RET_HEADS, 2 * RET_DK, RET_DV), F32)],
        compiler_params=pltpu.CompilerParams(dimension_semantics=("parallel", "arbitrary"),
                                             vmem_limit_bytes=VMEM_LIMIT),
        name="retention_bwd" if backward else "retention_fwd",
    )(*args)


def rope_partner_columns(w):
    w2 = w.reshape(w.shape[0], -1, 2)
    return jnp.stack([-w2[..., 1], w2[..., 0]], axis=-1).reshape(w.shape)


def rope_tables(cos, sin):
    c = jnp.tile(jnp.repeat(cos, 2, axis=1), (1, RET_HEADS))
    s = jnp.tile(jnp.repeat(sin, 2, axis=1), (1, RET_HEADS))
    return c, s


FFT_R = 128
FFT_N = FFT_R * FFT_R
FFT_L = FFT_N // 2
HY_CB = 16
HY_TAPS_CB = 64


def _dft_constants():
    r = np.arange(FFT_R)
    n1 = np.arange(FFT_R // 2)
    fa = np.exp(-2j * np.pi * np.outer(r, n1) / FFT_R)
    ma = np.block([[fa.real, -fa.imag], [fa.imag, fa.real]])
    g = np.exp(-2j * np.pi * np.outer(r, r) / FFT_R)
    mb = np.block([[g.real, g.imag], [-g.imag, g.real]])
    mbi = np.block([[g.real, -g.imag], [g.imag, g.real]])
    p = np.conj(fa).T / FFT_N
    mai = np.block([[p.real, -p.imag], [p.imag, p.real]])
    tw = np.exp(-2j * np.pi * np.outer(r, r) / FFT_N)
    as_bf = lambda a: jnp.asarray(a, F32).astype(BF)
    return (as_bf(ma), as_bf(mb), as_bf(mbi), as_bf(mai),
            jnp.asarray(tw.real, F32), jnp.asarray(tw.imag, F32))


def _fft_forward(z_scr, ya_scr, x_scr, ma_ref, mb_ref, tr_ref, ti_ref, nch):
    def body(c, carry):
        y = jnp.dot(ma_ref[...], z_scr[c].astype(BF), preferred_element_type=F32)
        ar, ai = y[:FFT_R], y[FFT_R:]
        rows = pl.ds(pl.multiple_of(c * FFT_R, FFT_R), FFT_R)
        ya_scr[rows, 0:FFT_R] = (ar * tr_ref[...] - ai * ti_ref[...]).astype(BF)
        ya_scr[rows, FFT_R:2 * FFT_R] = (ar * ti_ref[...] + ai * tr_ref[...]).astype(BF)
        return carry
    lax.fori_loop(0, nch, body, 0)
    x_scr[...] = jnp.dot(ya_scr[...], mb_ref[...], preferred_element_type=F32)


def _hy_conv_body(v_ref, x1_ref, x2_ref, h_ref, sw_ref, sb_ref, bias_ref,
                  ma_ref, mb_ref, mbi_ref, mai_ref, tr_ref, ti_ref, o_ref,
                  z_scr, ya_scr, x_scr, xb_scr, b_scr):
    nch = v_ref.shape[0]
    half = FFT_R // 2
    row = lax.broadcasted_iota(jnp.int32, (FFT_R, FFT_R), 0)
    lane = lax.broadcasted_iota(jnp.int32, (FFT_R, FFT_R), 1)
    first = lane == 0
    last = lane == FFT_R - 1
    seq_start = jnp.logical_and(first, row % half == 0)
    seq_end = jnp.logical_and(last, row % half == half - 1)

    def short_conv(x, s, c):
        a = pltpu.roll(x, 1, 1)
        prev = jnp.where(seq_start, 0.0, jnp.where(first, pltpu.roll(a, 1, 0), a))
        b = pltpu.roll(x, FFT_R - 1, 1)
        nxt = jnp.where(seq_end, 0.0, jnp.where(last, pltpu.roll(b, FFT_R - 1, 0), b))
        return sw_ref[s, 0, c] * prev + sw_ref[s, 1, c] * x + sw_ref[s, 2, c] * nxt + sb_ref[s, c]

    def load_body(c, carry):
        z_scr[c] = short_conv(v_ref[c], 0, c)
        return carry
    lax.fori_loop(0, nch, load_body, 0)

    for o, gate_ref in enumerate((x1_ref, x2_ref)):
        _fft_forward(z_scr, ya_scr, x_scr, ma_ref, mb_ref, tr_ref, ti_ref, nch)

        def filt_body(c, carry):
            rows = pl.ds(pl.multiple_of(c * FFT_R, FFT_R), FFT_R)
            xr, xi = x_scr[rows, 0:FFT_R], x_scr[rows, FFT_R:2 * FFT_R]
            hr, hi = h_ref[o, c, :, 0:FFT_R], h_ref[o, c, :, FFT_R:2 * FFT_R]
            xb_scr[rows, 0:FFT_R] = (xr * hr - xi * hi).astype(BF)
            xb_scr[rows, FFT_R:2 * FFT_R] = (xr * hi + xi * hr).astype(BF)
            return carry
        lax.fori_loop(0, nch, filt_body, 0)
        b_scr[...] = jnp.dot(xb_scr[...], mbi_ref[...], preferred_element_type=F32)

        def inv_body(c, carry):
            rows = pl.ds(pl.multiple_of(c * FFT_R, FFT_R), FFT_R)
            br, bi = b_scr[rows, 0:FFT_R], b_scr[rows, FFT_R:2 * FFT_R]
            rhs = jnp.concatenate([(br * tr_ref[...] + bi * ti_ref[...]).astype(BF),
                                   (bi * tr_ref[...] - br * ti_ref[...]).astype(BF)], axis=0)
            y = jnp.dot(mai_ref[...], rhs, preferred_element_type=F32)
            gate = short_conv(gate_ref[c], o + 1, c)
            z_scr[c] = gate * (y + z_scr[c] * bias_ref[o, c])
            return carry
        lax.fori_loop(0, nch, inv_body, 0)

    o_ref[...] = z_scr[...]


def hyena_conv(ph3, h_filt, short_w, short_b, bias):
    C = HY_WIDTH
    cb = HY_CB
    consts = _dft_constants()
    per_ch = lambda a: a.reshape(a.shape[:-1] + (C, 1, 1)).astype(F32)
    sw = per_ch(short_w.reshape(HY_SHORT_TAPS, HY_ORDER + 1, C).transpose(1, 0, 2))
    sb = per_ch(short_b.reshape(HY_ORDER + 1, C))
    bs = per_ch(bias)
    nb = C // cb
    chan = lambda s: pl.BlockSpec((cb, FFT_R, FFT_R), lambda i: (s * nb + i, 0, 0))
    full = lambda a: pl.BlockSpec(a.shape, lambda i: (0,) * a.ndim)
    return pl.pallas_call(
        _hy_conv_body,
        grid=(nb,),
        in_specs=[chan(0), chan(1), chan(2),
                  pl.BlockSpec((HY_ORDER, cb, FFT_R, 2 * FFT_R), lambda i: (0, i, 0, 0)),
                  pl.BlockSpec((HY_ORDER + 1, HY_SHORT_TAPS, cb, 1, 1), lambda i: (0, 0, i, 0, 0)),
                  pl.BlockSpec((HY_ORDER + 1, cb, 1, 1), lambda i: (0, i, 0, 0)),
                  pl.BlockSpec((HY_ORDER, cb, 1, 1), lambda i: (0, i, 0, 0))]
                 + [full(a) for a in consts],
        out_specs=pl.BlockSpec((cb, FFT_R, FFT_R), lambda i: (i, 0, 0)),
        out_shape=jax.ShapeDtypeStruct((C, FFT_R, FFT_R), F32),
        scratch_shapes=[pltpu.VMEM((cb, FFT_R, FFT_R), F32),
                        pltpu.VMEM((cb * FFT_R, 2 * FFT_R), BF),
                        pltpu.VMEM((cb * FFT_R, 2 * FFT_R), F32),
                        pltpu.VMEM((cb * FFT_R, 2 * FFT_R), BF),
                        pltpu.VMEM((cb * FFT_R, 2 * FFT_R), F32)],
        compiler_params=pltpu.CompilerParams(dimension_semantics=("parallel",),
                                             vmem_limit_bytes=VMEM_LIMIT),
        name="hyena_conv",
    )(ph3, ph3, ph3, h_filt, sw, sb, bs, *consts)


def hyena_features(L):
    t = jnp.arange(L, dtype=F32)
    t_norm = t / max(L - 1, 1)
    w_ang = 2.0 * math.pi * t / L
    bands = jnp.linspace(1e-4, HY_BANDS - 1, HY_BANDS, dtype=F32)
    feat = jnp.concatenate([t_norm[:, None], jnp.cos(w_ang[:, None] * bands),
                            -jnp.sin(w_ang[:, None] * bands)], axis=-1)
    return feat.T, t_norm[None, :]


def hyena_deltas():
    return jnp.abs(jnp.linspace(math.log(HY_DECAY_TARGET) / HY_DECAY_LONG_PCT,
                                math.log(HY_DECAY_TARGET) / HY_DECAY_SHORT_PCT, HY_WIDTH, dtype=F32))


def _hy_taps_body(featT_ref, tn_ref, w1T_ref, b1_ref, w2T_ref, b2_ref, freq_ref, w3T_ref, b3_ref, delta_ref,
                  o_ref, h2_scr):
    hi = lax.Precision.HIGHEST

    @pl.when(jnp.logical_and(pl.program_id(0) == 0, pl.program_id(1) == 0))
    def _():
        h1 = jnp.sin(freq_ref[...] * (jnp.dot(w1T_ref[...], featT_ref[...], preferred_element_type=F32,
                                              precision=hi) + b1_ref[...]))
        h2_scr[...] = jnp.sin(freq_ref[...] * (jnp.dot(w2T_ref[...], h1, preferred_element_type=F32,
                                                       precision=hi) + b2_ref[...]))
    window = jnp.exp(-delta_ref[...] * tn_ref[...])
    taps = []
    for d in range(2):
        h3 = jnp.dot(w3T_ref[0, d], h2_scr[...], preferred_element_type=F32, precision=hi) + b3_ref[0, d]
        taps.append(h3 * window)
    lane = lax.broadcasted_iota(jnp.int32, taps[1].shape, 1)
    taps[1] = jnp.where(lane == 0, 0.0, taps[1])
    ssq = jnp.sum(taps[0] * taps[0] + taps[1] * taps[1], axis=1, keepdims=True)
    scale = lax.rsqrt(ssq + EPS)
    o_ref[0, 0] = taps[0] * scale
    o_ref[0, 1] = taps[1] * scale


def hyena_taps(L, w1, b1, freq, w2, b2, w3, b3):
    C = HY_WIDTH
    cb = HY_TAPS_CB
    featT, tn = hyena_features(L)
    col = lambda a: a.reshape(-1, 1).astype(F32)
    w3T = w3.T.reshape(HY_ORDER, 2, C, HY_HIDDEN).astype(F32)
    b3c = b3.reshape(HY_ORDER, 2, C, 1).astype(F32)
    full = lambda a: pl.BlockSpec(a.shape, lambda i, o: (0,) * a.ndim)
    args = [featT, tn, w1.T.astype(F32), col(b1), w2.T.astype(F32), col(b2), col(freq)]
    return pl.pallas_call(
        _hy_taps_body,
        grid=(C // cb, HY_ORDER),
        in_specs=[full(a) for a in args]
                 + [pl.BlockSpec((1, 2, cb, HY_HIDDEN), lambda i, o: (o, 0, i, 0)),
                    pl.BlockSpec((1, 2, cb, 1), lambda i, o: (o, 0, i, 0)),
                    pl.BlockSpec((cb, 1), lambda i, o: (i, 0))],
        out_specs=pl.BlockSpec((1, 2, cb, L), lambda i, o: (o, 0, i, 0)),
        out_shape=jax.ShapeDtypeStruct((HY_ORDER, 2, C, L), F32),
        scratch_shapes=[pltpu.VMEM((HY_HIDDEN, L), F32)],
        compiler_params=pltpu.CompilerParams(dimension_semantics=("arbitrary", "arbitrary"),
                                             vmem_limit_bytes=VMEM_LIMIT),
        name="hyena_taps",
    )(*args, w3T, b3c, col(hyena_deltas()))


def _hy_spec_body(t_ref, ma_ref, mb_ref, tr_ref, ti_ref, o_ref, z_scr, ya_scr, x_scr):
    nch = t_ref.shape[2]
    half = FFT_R // 2
    zero = jnp.zeros((nch, half, FFT_R), F32)
    z_scr[0:nch, 0:half, :] = t_ref[0, 0]
    z_scr[0:nch, half:FFT_R, :] = zero
    z_scr[nch:2 * nch, 0:half, :] = t_ref[0, 1]
    z_scr[nch:2 * nch, half:FFT_R, :] = zero
    _fft_forward(z_scr, ya_scr, x_scr, ma_ref, mb_ref, tr_ref, ti_ref, 2 * nch)
    n = nch * FFT_R
    o_ref[0, :, :, 0:FFT_R] = (x_scr[0:n, 0:FFT_R] + x_scr[n:2 * n, 0:FFT_R]).reshape(nch, FFT_R, FFT_R)
    o_ref[0, :, :, FFT_R:2 * FFT_R] = (x_scr[0:n, FFT_R:2 * FFT_R]
                                       - x_scr[n:2 * n, FFT_R:2 * FFT_R]).reshape(nch, FFT_R, FFT_R)


def hyena_spectra(taps):
    C = HY_WIDTH
    cb = HY_CB
    ma, mb, _, _, tr, ti = _dft_constants()
    t5 = taps.reshape(HY_ORDER, 2, C, FFT_R // 2, FFT_R)
    full = lambda a: pl.BlockSpec(a.shape, lambda o, i: (0,) * a.ndim)
    return pl.pallas_call(
        _hy_spec_body,
        grid=(HY_ORDER, C // cb),
        in_specs=[pl.BlockSpec((1, 2, cb, FFT_R // 2, FFT_R), lambda o, i: (o, 0, i, 0, 0))]
                 + [full(a) for a in (ma, mb, tr, ti)],
        out_specs=pl.BlockSpec((1, cb, FFT_R, 2 * FFT_R), lambda o, i: (o, i, 0, 0)),
        out_shape=jax.ShapeDtypeStruct((HY_ORDER, C, FFT_R, 2 * FFT_R), F32),
        scratch_shapes=[pltpu.VMEM((2 * cb, FFT_R, FFT_R), F32),
                        pltpu.VMEM((2 * cb * FFT_R, 2 * FFT_R), BF),
                        pltpu.VMEM((2 * cb * FFT_R, 2 * FFT_R), F32)],
        compiler_params=pltpu.CompilerParams(dimension_semantics=("parallel", "parallel"),
                                             vmem_limit_bytes=VMEM_LIMIT),
        name="hyena_spectra",
    )(t5, ma, mb, tr, ti)


def hyena_long(ph, lp):
    B, L, W = ph.shape
    assert B == 2 and L == FFT_L
    taps = hyena_taps(L, lp['hy_w1'], lp['hy_b1'], lp['hy_freq'], lp['hy_w2'], lp['hy_b2'],
                      lp['hy_w3'], lp['hy_b3'])
    spectra = hyena_spectra(taps)
    ph3 = ph.reshape(B * L, W).T.reshape(W, FFT_R, FFT_R)
    z3 = hyena_conv(ph3, spectra, lp['hy_short_w'], lp['hy_short_b'], lp['hy_bias'])
    return z3.reshape(HY_WIDTH, B * L).T


LOG2E = 1.4426950408889634
NEG_BIG = -1e30
PEER_ROUTE_TM = 512
PEER_TM = 512
PEER_TE = 1024
PEER_ROWS = PEER_TE // PEER_NKEYS
LANES = 128


def _batcher_pairs(lo, hi):
    def merge(lo, hi, r):
        step = r * 2
        if step < hi - lo:
            yield from merge(lo, hi, step)
            yield from merge(lo + r, hi, step)
            yield from [(i, i + r) for i in range(lo + r, hi - r, step)]
        else:
            yield (lo, lo + r)
    if hi - lo >= 1:
        mid = lo + (hi - lo) // 2
        yield from _batcher_pairs(lo, mid)
        yield from _batcher_pairs(mid + 1, hi)
        yield from merge(lo, hi, 1)


SORT16 = tuple(_batcher_pairs(0, PEER_TOPK - 1))


def _cmpx(v, i, j):
    a, b = v[i], v[j]
    v[i] = jnp.maximum(a, b)
    v[j] = jnp.minimum(a, b)


def _bitonic_desc(v):
    for dist in (8, 4, 2, 1):
        for i in range(PEER_TOPK):
            if not i & dist:
                _cmpx(v, i, i + dist)


def _merge_top(cur, other, dropped):
    out = list(cur)
    for r, o in enumerate(other):
        i = PEER_TOPK - 1 - r
        out[i] = jnp.maximum(cur[i], o)
        dropped = jnp.maximum(dropped, jnp.minimum(cur[i], o))
    _bitonic_desc(out)
    return out, dropped


def _top17_of_rows(s):
    v = [s[8 * k:8 * k + 8, :] for k in range(PEER_NKEYS // 8)]
    for i, j in SORT16:
        _cmpx(v, i, j)
    dropped = jnp.full(v[0].shape, NEG_BIG, jnp.float32)
    for shift in (4, 2, 1):
        other = [pltpu.roll(x, shift, 0) for x in v]
        dropped = jnp.maximum(dropped, pltpu.roll(dropped, shift, 0))
        v, dropped = _merge_top(v, other, dropped)
    return v + [dropped]


def _peer_route_body(hT_ref, wqT_ref, keys_ref, s1p_ref, thr_ref, s2_ref):
    K = PEER_TOPK
    half = PEER_DKEY // 2
    qT = jnp.dot(wqT_ref[...], hT_ref[...], preferred_element_type=jnp.float32)
    s1 = jnp.dot(keys_ref[0], qT[:half], preferred_element_type=jnp.float32,
                 precision=lax.Precision.HIGHEST)
    s2 = jnp.dot(keys_ref[1], qT[half:], preferred_element_type=jnp.float32,
                 precision=lax.Precision.HIGHEST)
    t1 = _top17_of_rows(s1)
    t2 = _top17_of_rows(s2)
    best = [t1[0] + t2[b] for b in range(K)]
    dropped = t1[0] + t2[K]
    for a in range(1, 8):
        best, dropped = _merge_top(best, [t1[a] + t2[b] for b in range((K + 1) // (a + 1))], dropped)
    best, dropped = _merge_top(best, [t1[a] + t2[0] for a in range(8, K + 1)], dropped)
    tau = 0.5 * (best[K - 1] + dropped)
    z = jnp.zeros_like(tau)
    for r in range(K):
        z = z + jnp.exp(best[r] - best[0])
    lse = best[0] + jnp.log(z)
    s1p_ref[0] = (s1 - lse[0:1]) * LOG2E
    thr_ref[0] = (tau[0:1] - s1) * LOG2E
    s2_ref[0] = s2 * LOG2E


def peer_route(hT, wqT, keys):
    D, T = hT.shape
    tm = min(PEER_ROUTE_TM, T)
    assert T % tm == 0
    shp = jax.ShapeDtypeStruct((PEER_HEADS, PEER_NKEYS, T), jnp.float32)
    ospec = pl.BlockSpec((1, PEER_NKEYS, tm), lambda t, h: (h, 0, t))
    return pl.pallas_call(
        _peer_route_body,
        grid=(T // tm, PEER_HEADS),
        in_specs=[pl.BlockSpec((D, tm), lambda t, h: (0, t)),
                  pl.BlockSpec((PEER_DKEY, D), lambda t, h: (h, 0)),
                  pl.BlockSpec((2, PEER_NKEYS, PEER_DKEY // 2), lambda t, h: (h, 0, 0))],
        out_specs=[ospec, ospec, ospec],
        out_shape=[shp, shp, shp],
        compiler_params=pltpu.CompilerParams(dimension_semantics=("parallel", "arbitrary")),
        name="peer_route",
    )(hT, wqT, keys.reshape(PEER_HEADS * 2, PEER_NKEYS, PEER_DKEY // 2))


def _gelu_tanh(a):
    inner = a * (0.7978845608028654 + 0.035677408136300125 * (a * a))
    half = 0.5 * a
    return half + half * jnp.tanh(inner)


def _peer_expert_body(u_ref, hT_ref, vT_ref, s1p_ref, thr_ref, s2_ref, o_ref, a_scr, w_scr):
    tm = hT_ref.shape[1]
    a_scr[...] = jnp.dot(u_ref[...], hT_ref[...], preferred_element_type=jnp.float32)

    for r in range(PEER_ROWS):
        for g in range(tm // LANES):
            lanes = pl.ds(g * LANES, LANES)
            s1b = [jnp.broadcast_to(s1p_ref[h, r:r + 1, lanes], (16, LANES))
                   for h in range(PEER_HEADS)]
            thb = [jnp.broadcast_to(thr_ref[h, r:r + 1, lanes], (16, LANES))
                   for h in range(PEER_HEADS)]
            for jb in range(PEER_NKEYS // 16):
                rows = pl.ds(r * PEER_NKEYS + jb * 16, 16)
                gate = jnp.zeros((16, LANES), jnp.float32)
                for h in range(PEER_HEADS):
                    s2 = s2_ref[h, jb * 16:(jb + 1) * 16, lanes]
                    gate = gate + jnp.where(s2 >= thb[h], jnp.exp2(s2 + s1b[h]), 0.0)
                w_scr[rows, lanes] = (_gelu_tanh(a_scr[rows, lanes]) * gate).astype(jnp.bfloat16)
    contrib = jnp.dot(vT_ref[...], w_scr[...], preferred_element_type=jnp.float32)

    @pl.when(pl.program_id(1) == 0)
    def _():
        o_ref[...] = contrib

    @pl.when(pl.program_id(1) != 0)
    def _():
        o_ref[...] += contrib


def peer_experts(hT, u_bf, vT_bf, s1p, thr, s2):
    D, T = hT.shape
    E = u_bf.shape[0]
    tm = min(PEER_TM, T)
    te = PEER_TE
    assert T % tm == 0 and E % te == 0
    return pl.pallas_call(
        _peer_expert_body,
        grid=(T // tm, E // te),
        in_specs=[pl.BlockSpec((te, D), lambda t, e: (e, 0)),
                  pl.BlockSpec((D, tm), lambda t, e: (0, t)),
                  pl.BlockSpec((D, te), lambda t, e: (0, e)),
                  pl.BlockSpec((PEER_HEADS, PEER_ROWS, tm), lambda t, e: (0, e, t)),
                  pl.BlockSpec((PEER_HEADS, PEER_ROWS, tm), lambda t, e: (0, e, t)),
                  pl.BlockSpec((PEER_HEADS, PEER_NKEYS, tm), lambda t, e: (0, 0, t))],
        out_specs=pl.BlockSpec((D, tm), lambda t, e: (0, t)),
        out_shape=jax.ShapeDtypeStruct((D, T), jnp.float32),
        scratch_shapes=[pltpu.VMEM((te, tm), jnp.float32),
                        pltpu.VMEM((te, tm), jnp.bfloat16)],
        compiler_params=pltpu.CompilerParams(dimension_semantics=("parallel", "arbitrary"),
                                             vmem_limit_bytes=48 * 1024 * 1024),
        name="peer_experts",
    )(u_bf, hT, vT_bf, s1p, thr, s2)


def peer_ffn(h, w_q, sub_keys, u_bf, vT_bf):
    B, L, D = h.shape
    hT = h.reshape(B * L, D).T.astype(jnp.bfloat16)
    s1p, thr, s2 = peer_route(hT, w_q.T.astype(jnp.bfloat16), sub_keys.astype(jnp.float32))
    outT = peer_experts(hT, u_bf, vT_bf, s1p, thr, s2)
    return outT.T.reshape(B, L, D)


def kernel(x, c, ctx, c_ctx, ada_w, ada_b, norm1_g, norm2_g, w_in, ret_decay_logit, ret_gn_g, w_br_ret, conf_dw_w, conf_dw_b, conf_ln_g, conf_ln_b, w_br_conf, hy_short_w, hy_short_b, hy_w1, hy_b1, hy_freq, hy_w2, hy_b2, hy_w3, hy_b3, hy_bias, w_br_hy, w_out, peer_wq, peer_keys, peer_u, peer_v, final_g):
    depth = ada_w.shape[0]
    B, L, _ = x.shape
    Lc = ctx.shape[1]
    HK = RET_HEADS * RET_DK
    HV = RET_HEADS * RET_DV
    ret_cols = dict(col_q=OFF_Q // HK, col_k=OFF_K // HK, col_qp=IN_COLS // HK, col_kp=IN_COLS // HK + 1,
                    col_v=OFF_V // HV)
    cosf, sinf = rope_tables(*grid_rope_angles(L // GRID_W))
    cosf_c, sinf_c = jnp.ones((Lc, HK), F32), jnp.zeros((Lc, HK), F32)
    zero_state = jnp.zeros((B, RET_HEADS, 2 * RET_DK, RET_DV), F32)

    def retention(p, seq, cos_t, sin_t, tabs, s0_f, s0_b, gn_g):
        p2 = p.reshape(B * seq, -1)
        o_f, fin_f = retention_pass(p2, B, seq, cos_t, sin_t, tabs[0], s0_f, backward=False, **ret_cols)
        pre, fin_b = retention_pass(p2, B, seq, cos_t, sin_t, tabs[1], s0_b, backward=True, o_fwd=o_f,
                                    col_g=OFF_G // HV, gn_g=gn_g, **ret_cols)
        return pre, fin_f, fin_b

    for l in range(depth):
        last = l == depth - 1
        lp = {
            'log_g': jax.nn.log_sigmoid(ret_decay_logit[l].astype(jnp.float32)),
            'ret_gn_g': ret_gn_g[l], 'w_br_ret': w_br_ret[l],
            'conf_dw_w': conf_dw_w[l], 'conf_dw_b': conf_dw_b[l],
            'conf_ln_g': conf_ln_g[l], 'conf_ln_b': conf_ln_b[l], 'w_br_conf': w_br_conf[l],
            'hy_short_w': hy_short_w[l], 'hy_short_b': hy_short_b[l],
            'hy_w1': hy_w1[l], 'hy_b1': hy_b1[l], 'hy_freq': hy_freq[l],
            'hy_w2': hy_w2[l], 'hy_b2': hy_b2[l], 'hy_w3': hy_w3[l], 'hy_b3': hy_b3[l],
            'hy_bias': hy_bias[l], 'w_br_hy': w_br_hy[l], 'w_out': w_out[l],
        }
        u_bf = peer_u[l].astype(jnp.bfloat16)
        vT_bf = peer_v[l].T.astype(jnp.bfloat16)
        mod = adaln(c, ada_w[l], ada_b[l])
        mod_c = adaln(c_ctx[None, :], ada_w[l], ada_b[l])

        w_ext = jnp.concatenate([w_in[l], rope_partner_columns(w_in[l][:, OFF_Q:OFF_K]),
                                 rope_partner_columns(w_in[l][:, OFF_K:OFF_V])], axis=1)
        tabs = (retention_tables(lp['log_g'][0], False), retention_tables(lp['log_g'][1], True))

        u_c = modulate(rms_norm(ctx, norm1_g[l]), mod_c[0], mod_c[1])
        p_c = matmul(u_c, w_ext)
        pre_c, s_f, s_b = retention(p_c, Lc, cosf_c, sinf_c, tabs, zero_state, zero_state, lp['ret_gn_g'])

        u = modulate(rms_norm(x, norm1_g[l]), mod[0], mod[1])
        p = matmul(u, w_ext)
        pre, _, _ = retention(p, L, cosf, sinf, tabs, s_f, s_b, lp['ret_gn_g'])
        x = x + mod[2] * mixer_block(p, pre, lp)

        x = x + mod[5] * peer_ffn(modulate(rms_norm(x, norm2_g[l]), mod[3], mod[4]),
                                  peer_wq[l], peer_keys[l], u_bf, vT_bf)

        if not last:
            ctx = ctx + mod_c[2] * mixer_block(p_c, pre_c, lp)
            ctx = ctx + mod_c[5] * peer_ffn(modulate(rms_norm(ctx, norm2_g[l]), mod_c[3], mod_c[4]),
                                            peer_wq[l], peer_keys[l], u_bf, vT_bf)
    return rms_norm(x, final_g)
```

```python
import functools
import math

import jax
import jax.numpy as jnp
import numpy as np
from jax import lax
from jax.experimental import pallas as pl
from jax.experimental.pallas import tpu as pltpu

D_MODEL = 1024
GRID_W = 64
EPS = 1e-6

RET_HEADS = 8
RET_DV = D_MODEL // RET_HEADS
RET_DK = RET_DV // 2
RET_CHUNK = 128
ROPE_BASE = 10000.0

CONF_WIDTH = D_MODEL // 2
CONF_TAPS = 31

HY_WIDTH = D_MODEL // 2
HY_ORDER = 2
HY_SHORT_TAPS = 3
HY_BANDS = 16
HY_EMB = 2 * HY_BANDS + 1
HY_HIDDEN = 64
HY_DECAY_SHORT_PCT = 0.3
HY_DECAY_LONG_PCT = 1.5
HY_DECAY_TARGET = 1e-2

PEER_HEADS = 8
PEER_NKEYS = 128
PEER_DKEY = D_MODEL // PEER_HEADS
PEER_TOPK = 16
PEER_TOKEN_BLOCK = 128

N_BRANCH = 3
OFF_Q = 0
OFF_K = OFF_Q + RET_HEADS * RET_DK
OFF_V = OFF_K + RET_HEADS * RET_DK
OFF_G = OFF_V + RET_HEADS * RET_DV
OFF_CONF = OFF_G + RET_HEADS * RET_DV
OFF_HY = OFF_CONF + 2 * CONF_WIDTH
OFF_GATE = OFF_HY + (HY_ORDER + 1) * HY_WIDTH
IN_COLS = OFF_GATE + N_BRANCH * D_MODEL


INPROJ_TM = 512
INPROJ_COL_BLOCKS = 4


def _inproj_body(x_ref, g_ref, shift_ref, scale_ref, w_ref, o_ref, u_scr):
    @pl.when(pl.program_id(1) == 0)
    def _():
        x = x_ref[...]
        y = x * lax.rsqrt(jnp.mean(x * x, axis=-1, keepdims=True) + EPS) * g_ref[...]
        u_scr[...] = (y * (1.0 + scale_ref[0]) + shift_ref[0]).astype(jnp.bfloat16)

    o_ref[...] = jnp.dot(u_scr[...], w_ref[...], preferred_element_type=jnp.float32)


def norm_project(x, gain, shift, scale, w_bf, n_col_blocks):
    B, L, D = x.shape
    N = w_bf.shape[1]
    tm = min(INPROJ_TM, L)
    assert L % tm == 0 and N % n_col_blocks == 0
    tn = N // n_col_blocks
    tiles_per_seq = L // tm
    per_batch = shift.shape[0] > 1
    mod_spec = pl.BlockSpec((1, 1, D), lambda i, j: (i // tiles_per_seq if per_batch else 0, 0, 0))
    out = pl.pallas_call(
        _inproj_body,
        grid=(B * L // tm, n_col_blocks),
        in_specs=[pl.BlockSpec((tm, D), lambda i, j: (i, 0)),
                  pl.BlockSpec((1, D), lambda i, j: (0, 0)),
                  mod_spec, mod_spec,
                  pl.BlockSpec((D, tn), lambda i, j: (0, j))],
        out_specs=pl.BlockSpec((tm, tn), lambda i, j: (i, j)),
        out_shape=jax.ShapeDtypeStruct((B * L, N), jnp.float32),
        scratch_shapes=[pltpu.VMEM((tm, D), jnp.bfloat16)],
        compiler_params=pltpu.CompilerParams(dimension_semantics=("parallel", "arbitrary"),
                                             vmem_limit_bytes=48 * 1024 * 1024),
        name="norm_project",
    )(x.reshape(B * L, D), gain.reshape(1, D).astype(jnp.float32), shift, scale, w_bf)
    return out.reshape(B, L, N)


def rms_norm(x, gain):
    xf = x.astype(jnp.float32)
    y = xf * lax.rsqrt(jnp.mean(xf * xf, axis=-1, keepdims=True) + EPS)
    return (y * gain.astype(jnp.float32)).astype(x.dtype)


def modulate(h, shift, scale):
    return h * (1 + scale) + shift


def adaln(cvec, w, b):
    m = (jax.nn.silu(cvec) @ w + b)[:, None, :]
    return jnp.split(m, 6, axis=-1)


def depthwise_conv(x, w, b):
    taps = w.shape[0]
    left = (taps - 1) // 2
    y = lax.conv_general_dilated(
        x, w[:, None, :].astype(x.dtype), window_strides=(1,),
        padding=[(left, taps - 1 - left)],
        dimension_numbers=('NWC', 'WIO', 'NWC'),
        feature_group_count=x.shape[-1])
    return y + b.astype(x.dtype)


def grid_rope_angles(rows):
    rr, cc = jnp.meshgrid(jnp.arange(rows, dtype=jnp.float32),
                          jnp.arange(GRID_W, dtype=jnp.float32), indexing='ij')
    row = rr.reshape(-1)
    col = cc.reshape(-1)
    n_freq = RET_DK // 4
    inv = ROPE_BASE ** (-jnp.arange(n_freq, dtype=jnp.float32) / n_freq)
    ang = jnp.concatenate([row[:, None] * inv, col[:, None] * inv], axis=-1)
    return jnp.cos(ang), jnp.sin(ang)


def hyena_filters(L, lp):
    f32 = jnp.float32
    t = jnp.arange(L, dtype=f32)
    t_norm = t / max(L - 1, 1)
    w_ang = 2.0 * math.pi * t / L
    bands = jnp.linspace(1e-4, HY_BANDS - 1, HY_BANDS, dtype=f32)
    feat = jnp.concatenate([t_norm[:, None], jnp.cos(w_ang[:, None] * bands),
                            -jnp.sin(w_ang[:, None] * bands)], axis=-1)
    freq = lp['hy_freq'].astype(f32)
    h = jnp.sin(freq * (feat @ lp['hy_w1'].astype(f32) + lp['hy_b1'].astype(f32)))
    h = jnp.sin(freq * (h @ lp['hy_w2'].astype(f32) + lp['hy_b2'].astype(f32)))
    h = (h @ lp['hy_w3'].astype(f32) + lp['hy_b3'].astype(f32)).reshape(L, HY_ORDER, 2, HY_WIDTH)
    deltas = jnp.abs(jnp.linspace(math.log(HY_DECAY_TARGET) / HY_DECAY_LONG_PCT,
                                  math.log(HY_DECAY_TARGET) / HY_DECAY_SHORT_PCT, HY_WIDTH, dtype=f32))
    window = jnp.exp(-t_norm[:, None] * deltas[None, :])
    h = h * window[:, None, None, :]
    h_fwd = h[:, :, 0]
    h_bwd = h[:, :, 1]
    taps = jnp.concatenate([h_fwd, jnp.zeros_like(h_fwd[:1]), jnp.flip(h_bwd[1:], axis=0)], axis=0)
    taps = taps * lax.rsqrt(jnp.sum(taps * taps, axis=0, keepdims=True) + EPS)
    return jnp.fft.rfft(taps, axis=0)


def hyena_branch(ph, filt_fft, lp):
    L = ph.shape[1]
    ph = depthwise_conv(ph, lp['hy_short_w'], lp['hy_short_b'])
    v, x1, x2 = jnp.split(ph, 3, axis=-1)
    z = v.astype(jnp.float32)
    bias = lp['hy_bias'].astype(jnp.float32)
    for o, gate in enumerate((x1, x2)):
        zf = jnp.fft.rfft(z, n=2 * L, axis=1)
        y = jnp.fft.irfft(zf * filt_fft[None, :, o, :], n=2 * L, axis=1)[:, :L]
        z = gate.astype(jnp.float32) * (y + z * bias[o])
    return z


BF = jnp.bfloat16
F32 = jnp.float32
RET_TL = 256
VMEM_LIMIT = 48 * 1024 * 1024


def retention_tables(log_g, backward):
    C = RET_CHUNK
    idx = jnp.arange(C, dtype=F32)
    diff = idx[:, None] - idx[None, :]
    if backward:
        diff = -diff
        qe = C - idx
        ke = idx
    else:
        qe = idx + 1
        ke = C - 1 - idx
    dmask = jnp.where(diff[None] >= 0, jnp.exp(jnp.maximum(diff, 0.0)[None] * log_g[:, None, None]), 0.0)
    qdec = jnp.repeat(jnp.exp(qe[:, None] * log_g[None, :]), RET_DK, axis=1)
    kdec = jnp.repeat(jnp.exp(ke[:, None] * log_g[None, :]), RET_DK, axis=1)
    gc = jnp.broadcast_to(jnp.exp(C * log_g)[:, None, None], (RET_HEADS, 2 * RET_DK, RET_DV))
    return dmask, qdec, kdec, gc


def _ret_body(*refs, backward, nchunk, epilogue):
    if epilogue:
        (q_ref, k_ref, qp_ref, kp_ref, v_ref, cos_ref, sin_ref, dmask_ref, qdec_ref, kdec_ref, gc_ref,
         s0_ref, of_ref, g_ref, gn_ref, o_ref, sfin_ref, s_scr) = refs
    else:
        (q_ref, k_ref, qp_ref, kp_ref, v_ref, cos_ref, sin_ref, dmask_ref, qdec_ref, kdec_ref, gc_ref,
         s0_ref, o_ref, sfin_ref, s_scr) = refs
    C = RET_CHUNK

    @pl.when(pl.program_id(1) == 0)
    def _():
        s_scr[...] = s0_ref[0]

    low = lax.broadcasted_iota(jnp.int32, (C, 2 * RET_DK), 1) < RET_DK
    for ci in range(nchunk):
        c = nchunk - 1 - ci if backward else ci
        rows = slice(c * C, (c + 1) * C)
        cosv = cos_ref[rows, :]
        sinv = sin_ref[rows, :]
        q = q_ref[rows, :] * cosv + qp_ref[rows, :] * sinv
        k = (k_ref[rows, :] * cosv + kp_ref[rows, :] * sinv) * (RET_DK ** -0.5)
        qd = q * qdec_ref[...]
        kd = k * kdec_ref[...]
        for h in range(RET_HEADS):
            grp = slice((h // 2) * 128, (h // 2 + 1) * 128)
            keep = low if h % 2 == 0 else jnp.logical_not(low)
            hv = slice(h * RET_DV, (h + 1) * RET_DV)
            qh = q[:, grp].astype(BF)
            kh = jnp.where(keep, k[:, grp], 0.0).astype(BF)
            kdh = jnp.where(keep, kd[:, grp], 0.0).astype(BF)
            vh = v_ref[rows, hv].astype(BF)
            sc = lax.dot_general(qh, kh, (((1,), (1,)), ((), ())), preferred_element_type=F32)
            sc = sc * dmask_ref[h]
            o = jnp.dot(sc.astype(BF), vh, preferred_element_type=F32)
            o = o + jnp.dot(qd[:, grp].astype(BF), s_scr[h].astype(BF), preferred_element_type=F32)
            kv = lax.dot_general(kdh, vh, (((0,), (0,)), ((), ())), preferred_element_type=F32)
            s_scr[h] = gc_ref[h] * s_scr[h] + kv
            if epilogue:
                o = o + of_ref[rows, hv]
                mu = jnp.mean(o, axis=-1, keepdims=True)
                d = o - mu
                var = jnp.mean(d * d, axis=-1, keepdims=True)
                o = d * lax.rsqrt(var + EPS) * gn_ref[:, hv]
                gg = g_ref[rows, hv]
                o = o * (gg * jax.nn.sigmoid(gg))
            o_ref[rows, hv] = o.astype(o_ref.dtype)

    @pl.when(pl.program_id(1) == pl.num_programs(1) - 1)
    def _():
        sfin_ref[0] = s_scr[...]


def retention_pass(p, B, L, cosf, sinf, tables, s0, *, backward, col_q, col_k, col_qp, col_kp, col_v,
                   o_fwd=None, col_g=None, gn_g=None):
    epilogue = o_fwd is not None
    tl = min(RET_TL, L)
    nt = L // tl
    dmask, qdec, kdec, gc = tables
    HK = RET_HEADS * RET_DK
    HV = RET_HEADS * RET_DV

    def pos(b, n):
        return nt - 1 - n if backward else n

    def tok(b, n):
        return b * nt + pos(b, n)

    def colspec(width, col):
        return pl.BlockSpec((tl, width), lambda b, n: (tok(b, n), col))

    const3 = lambda shape: pl.BlockSpec(shape, lambda b, n: (0, 0, 0))
    const2 = lambda shape: pl.BlockSpec(shape, lambda b, n: (0, 0))
    state = pl.BlockSpec((1, RET_HEADS, 2 * RET_DK, RET_DV), lambda b, n: (b, 0, 0, 0))
    in_specs = [colspec(HK, col_q), colspec(HK, col_k), colspec(HK, col_qp), colspec(HK, col_kp),
                colspec(HV, col_v),
                pl.BlockSpec((tl, HK), lambda b, n: (pos(b, n), 0)),
                pl.BlockSpec((tl, HK), lambda b, n: (pos(b, n), 0)),
                const3((RET_HEADS, RET_CHUNK, RET_CHUNK)), const2((RET_CHUNK, HK)), const2((RET_CHUNK, HK)),
                const3((RET_HEADS, 2 * RET_DK, RET_DV)), state]
    args = [p, p, p, p, p, cosf, sinf, dmask, qdec, kdec, gc, s0]
    if epilogue:
        in_specs += [pl.BlockSpec((tl, HV), lambda b, n: (tok(b, n), 0)), colspec(HV, col_g), const2((1, HV))]
        args += [o_fwd, p, gn_g.reshape(1, HV).astype(F32)]
    return pl.pallas_call(
        functools.partial(_ret_body, backward=backward, nchunk=tl // RET_CHUNK, epilogue=epilogue),
        grid=(B, nt),
        in_specs=in_specs,
        out_specs=[pl.BlockSpec((tl, HV), lambda b, n: (tok(b, n), 0)), state],
        out_shape=[jax.ShapeDtypeStruct((B * L, HV), BF if epilogue else F32),
                   jax.ShapeDtypeStruct((B, RET_HEADS, 2 * RET_DK, RET_DV), F32)],
        scratch_shapes=[pltpu.VMEM((RET_HEADS, 2 * RET_DK, RET_DV), F32)],
        compiler_params=pltpu.CompilerParams(dimension_semantics=("parallel", "arbitrary"),
                                             vmem_limit_bytes=VMEM_LIMIT),
        name="retention_bwd" if backward else "retention_fwd",
    )(*args)


def rope_partner_columns(w):
    w2 = w.reshape(w.shape[0], -1, 2)
    return jnp.stack([-w2[..., 1], w2[..., 0]], axis=-1).reshape(w.shape)


def rope_tables(cos, sin):
    c = jnp.tile(jnp.repeat(cos, 2, axis=1), (1, RET_HEADS))
    s = jnp.tile(jnp.repeat(sin, 2, axis=1), (1, RET_HEADS))
    return c, s


FFT_R = 128
FFT_N = FFT_R * FFT_R
FFT_L = FFT_N // 2
HY_CB = 16
HY_TAPS_CB = 64


def _dft_constants():
    r = np.arange(FFT_R)
    n1 = np.arange(FFT_R // 2)
    fa = np.exp(-2j * np.pi * np.outer(r, n1) / FFT_R)
    ma = np.block([[fa.real, -fa.imag], [fa.imag, fa.real]])
    g = np.exp(-2j * np.pi * np.outer(r, r) / FFT_R)
    mb = np.block([[g.real, g.imag], [-g.imag, g.real]])
    mbi = np.block([[g.real, -g.imag], [g.imag, g.real]])
    p = np.conj(fa).T / FFT_N
    mai = np.block([[p.real, -p.imag], [p.imag, p.real]])
    tw = np.exp(-2j * np.pi * np.outer(r, r) / FFT_N)
    as_bf = lambda a: jnp.asarray(a, F32).astype(BF)
    return (as_bf(ma), as_bf(mb), as_bf(mbi), as_bf(mai),
            jnp.asarray(tw.real, F32), jnp.asarray(tw.imag, F32))


def _fft_forward(z_scr, ya_scr, x_scr, ma_ref, mb_ref, tr_ref, ti_ref, nch):
    def body(c, carry):
        y = jnp.dot(ma_ref[...], z_scr[c].astype(BF), preferred_element_type=F32)
        ar, ai = y[:FFT_R], y[FFT_R:]
        rows = pl.ds(pl.multiple_of(c * FFT_R, FFT_R), FFT_R)
        ya_scr[rows, 0:FFT_R] = (ar * tr_ref[...] - ai * ti_ref[...]).astype(BF)
        ya_scr[rows, FFT_R:2 * FFT_R] = (ar * ti_ref[...] + ai * tr_ref[...]).astype(BF)
        return carry
    lax.fori_loop(0, nch, body, 0)
    x_scr[...] = jnp.dot(ya_scr[...], mb_ref[...], preferred_element_type=F32)


def _hy_conv_body(v_ref, x1_ref, x2_ref, h_ref, sw_ref, sb_ref, bias_ref,
                  ma_ref, mb_ref, mbi_ref, mai_ref, tr_ref, ti_ref, o_ref,
                  z_scr, ya_scr, x_scr, xb_scr, b_scr):
    nch = v_ref.shape[0]
    half = FFT_R // 2
    row = lax.broadcasted_iota(jnp.int32, (FFT_R, FFT_R), 0)
    lane = lax.broadcasted_iota(jnp.int32, (FFT_R, FFT_R), 1)
    first = lane == 0
    last = lane == FFT_R - 1
    seq_start = jnp.logical_and(first, row % half == 0)
    seq_end = jnp.logical_and(last, row % half == half - 1)

    def short_conv(x, s, c):
        a = pltpu.roll(x, 1, 1)
        prev = jnp.where(seq_start, 0.0, jnp.where(first, pltpu.roll(a, 1, 0), a))
        b = pltpu.roll(x, FFT_R - 1, 1)
        nxt = jnp.where(seq_end, 0.0, jnp.where(last, pltpu.roll(b, FFT_R - 1, 0), b))
        return sw_ref[s, 0, c] * prev + sw_ref[s, 1, c] * x + sw_ref[s, 2, c] * nxt + sb_ref[s, c]

    def load_body(c, carry):
        z_scr[c] = short_conv(v_ref[c], 0, c)
        return carry
    lax.fori_loop(0, nch, load_body, 0)

    for o, gate_ref in enumerate((x1_ref, x2_ref)):
        _fft_forward(z_scr, ya_scr, x_scr, ma_ref, mb_ref, tr_ref, ti_ref, nch)

        def filt_body(c, carry):
            rows = pl.ds(pl.multiple_of(c * FFT_R, FFT_R), FFT_R)
            xr, xi = x_scr[rows, 0:FFT_R], x_scr[rows, FFT_R:2 * FFT_R]
            hr, hi = h_ref[o, c, :, 0:FFT_R], h_ref[o, c, :, FFT_R:2 * FFT_R]
            xb_scr[rows, 0:FFT_R] = (xr * hr - xi * hi).astype(BF)
            xb_scr[rows, FFT_R:2 * FFT_R] = (xr * hi + xi * hr).astype(BF)
            return carry
        lax.fori_loop(0, nch, filt_body, 0)
        b_scr[...] = jnp.dot(xb_scr[...], mbi_ref[...], preferred_element_type=F32)

        def inv_body(c, carry):
            rows = pl.ds(pl.multiple_of(c * FFT_R, FFT_R), FFT_R)
            br, bi = b_scr[rows, 0:FFT_R], b_scr[rows, FFT_R:2 * FFT_R]
            rhs = jnp.concatenate([(br * tr_ref[...] + bi * ti_ref[...]).astype(BF),
                                   (bi * tr_ref[...] - br * ti_ref[...]).astype(BF)], axis=0)
            y = jnp.dot(mai_ref[...], rhs, preferred_element_type=F32)
            gate = short_conv(gate_ref[c], o + 1, c)
            z_scr[c] = gate * (y + z_scr[c] * bias_ref[o, c])
            return carry
        lax.fori_loop(0, nch, inv_body, 0)

    o_ref[...] = z_scr[...]


def hyena_conv(ph3, h_filt, short_w, short_b, bias):
    C = HY_WIDTH
    cb = HY_CB
    consts = _dft_constants()
    per_ch = lambda a: a.reshape(a.shape[:-1] + (C, 1, 1)).astype(F32)
    sw = per_ch(short_w.reshape(HY_SHORT_TAPS, HY_ORDER + 1, C).transpose(1, 0, 2))
    sb = per_ch(short_b.reshape(HY_ORDER + 1, C))
    bs = per_ch(bias)
    nb = C // cb
    chan = lambda s: pl.BlockSpec((cb, FFT_R, FFT_R), lambda i: (s * nb + i, 0, 0))
    full = lambda a: pl.BlockSpec(a.shape, lambda i: (0,) * a.ndim)
    return pl.pallas_call(
        _hy_conv_body,
        grid=(nb,),
        in_specs=[chan(0), chan(1), chan(2),
                  pl.BlockSpec((HY_ORDER, cb, FFT_R, 2 * FFT_R), lambda i: (0, i, 0, 0)),
                  pl.BlockSpec((HY_ORDER + 1, HY_SHORT_TAPS, cb, 1, 1), lambda i: (0, 0, i, 0, 0)),
                  pl.BlockSpec((HY_ORDER + 1, cb, 1, 1), lambda i: (0, i, 0, 0)),
                  pl.BlockSpec((HY_ORDER, cb, 1, 1), lambda i: (0, i, 0, 0))]
                 + [full(a) for a in consts],
        out_specs=pl.BlockSpec((cb, FFT_R, FFT_R), lambda i: (i, 0, 0)),
        out_shape=jax.ShapeDtypeStruct((C, FFT_R, FFT_R), F32),
        scratch_shapes=[pltpu.VMEM((cb, FFT_R, FFT_R), F32),
                        pltpu.VMEM((cb * FFT_R, 2 * FFT_R), BF),
                        pltpu.VMEM((cb * FFT_R, 2 * FFT_R), F32),
                        pltpu.VMEM((cb * FFT_R, 2 * FFT_R), BF),
                        pltpu.VMEM((cb * FFT_R, 2 * FFT_R), F32)],
        compiler_params=pltpu.CompilerParams(dimension_semantics=("parallel",),
                                             vmem_limit_bytes=VMEM_LIMIT),
        name="hyena_conv",
    )(ph3, ph3, ph3, h_filt, sw, sb, bs, *consts)


def hyena_features(L):
    t = jnp.arange(L, dtype=F32)
    t_norm = t / max(L - 1, 1)
    w_ang = 2.0 * math.pi * t / L
    bands = jnp.linspace(1e-4, HY_BANDS - 1, HY_BANDS, dtype=F32)
    feat = jnp.concatenate([t_norm[:, None], jnp.cos(w_ang[:, None] * bands),
                            -jnp.sin(w_ang[:, None] * bands)], axis=-1)
    return feat.T, t_norm[None, :]


def hyena_deltas():
    return jnp.abs(jnp.linspace(math.log(HY_DECAY_TARGET) / HY_DECAY_LONG_PCT,
                                math.log(HY_DECAY_TARGET) / HY_DECAY_SHORT_PCT, HY_WIDTH, dtype=F32))


def _hy_taps_body(featT_ref, tn_ref, w1T_ref, b1_ref, w2T_ref, b2_ref, freq_ref, w3T_ref, b3_ref, delta_ref,
                  o_ref, h2_scr):
    hi = lax.Precision.HIGHEST

    @pl.when(jnp.logical_and(pl.program_id(0) == 0, pl.program_id(1) == 0))
    def _():
        h1 = jnp.sin(freq_ref[...] * (jnp.dot(w1T_ref[...], featT_ref[...], preferred_element_type=F32,
                                              precision=hi) + b1_ref[...]))
        h2_scr[...] = jnp.sin(freq_ref[...] * (jnp.dot(w2T_ref[...], h1, preferred_element_type=F32,
                                                       precision=hi) + b2_ref[...]))
    window = jnp.exp(-delta_ref[...] * tn_ref[...])
    taps = []
    for d in range(2):
        h3 = jnp.dot(w3T_ref[0, d], h2_scr[...], preferred_element_type=F32, precision=hi) + b3_ref[0, d]
        taps.append(h3 * window)
    lane = lax.broadcasted_iota(jnp.int32, taps[1].shape, 1)
    taps[1] = jnp.where(lane == 0, 0.0, taps[1])
    ssq = jnp.sum(taps[0] * taps[0] + taps[1] * taps[1], axis=1, keepdims=True)
    scale = lax.rsqrt(ssq + EPS)
    o_ref[0, 0] = taps[0] * scale
    o_ref[0, 1] = taps[1] * scale


def hyena_taps(L, w1, b1, freq, w2, b2, w3, b3):
    C = HY_WIDTH
    cb = HY_TAPS_CB
    featT, tn = hyena_features(L)
    col = lambda a: a.reshape(-1, 1).astype(F32)
    w3T = w3.T.reshape(HY_ORDER, 2, C, HY_HIDDEN).astype(F32)
    b3c = b3.reshape(HY_ORDER, 2, C, 1).astype(F32)
    full = lambda a: pl.BlockSpec(a.shape, lambda i, o: (0,) * a.ndim)
    args = [featT, tn, w1.T.astype(F32), col(b1), w2.T.astype(F32), col(b2), col(freq)]
    return pl.pallas_call(
        _hy_taps_body,
        grid=(C // cb, HY_ORDER),
        in_specs=[full(a) for a in args]
                 + [pl.BlockSpec((1, 2, cb, HY_HIDDEN), lambda i, o: (o, 0, i, 0)),
                    pl.BlockSpec((1, 2, cb, 1), lambda i, o: (o, 0, i, 0)),
                    pl.BlockSpec((cb, 1), lambda i, o: (i, 0))],
        out_specs=pl.BlockSpec((1, 2, cb, L), lambda i, o: (o, 0, i, 0)),
        out_shape=jax.ShapeDtypeStruct((HY_ORDER, 2, C, L), F32),
        scratch_shapes=[pltpu.VMEM((HY_HIDDEN, L), F32)],
        compiler_params=pltpu.CompilerParams(dimension_semantics=("arbitrary", "arbitrary"),
                                             vmem_limit_bytes=VMEM_LIMIT),
        name="hyena_taps",
    )(*args, w3T, b3c, col(hyena_deltas()))


def _hy_spec_body(t_ref, ma_ref, mb_ref, tr_ref, ti_ref, o_ref, z_scr, ya_scr, x_scr):
    nch = t_ref.shape[2]
    half = FFT_R // 2
    zero = jnp.zeros((nch, half, FFT_R), F32)
    z_scr[0:nch, 0:half, :] = t_ref[0, 0]
    z_scr[0:nch, half:FFT_R, :] = zero
    z_scr[nch:2 * nch, 0:half, :] = t_ref[0, 1]
    z_scr[nch:2 * nch, half:FFT_R, :] = zero
    _fft_forward(z_scr, ya_scr, x_scr, ma_ref, mb_ref, tr_ref, ti_ref, 2 * nch)
    n = nch * FFT_R
    o_ref[0, :, :, 0:FFT_R] = (x_scr[0:n, 0:FFT_R] + x_scr[n:2 * n, 0:FFT_R]).reshape(nch, FFT_R, FFT_R)
    o_ref[0, :, :, FFT_R:2 * FFT_R] = (x_scr[0:n, FFT_R:2 * FFT_R]
                                       - x_scr[n:2 * n, FFT_R:2 * FFT_R]).reshape(nch, FFT_R, FFT_R)


def hyena_spectra(taps):
    C = HY_WIDTH
    cb = HY_CB
    ma, mb, _, _, tr, ti = _dft_constants()
    t5 = taps.reshape(HY_ORDER, 2, C, FFT_R // 2, FFT_R)
    full = lambda a: pl.BlockSpec(a.shape, lambda o, i: (0,) * a.ndim)
    return pl.pallas_call(
        _hy_spec_body,
        grid=(HY_ORDER, C // cb),
        in_specs=[pl.BlockSpec((1, 2, cb, FFT_R // 2, FFT_R), lambda o, i: (o, 0, i, 0, 0))]
                 + [full(a) for a in (ma, mb, tr, ti)],
        out_specs=pl.BlockSpec((1, cb, FFT_R, 2 * FFT_R), lambda o, i: (o, i, 0, 0)),
        out_shape=jax.ShapeDtypeStruct((HY_ORDER, C, FFT_R, 2 * FFT_R), F32),
        scratch_shapes=[pltpu.VMEM((2 * cb, FFT_R, FFT_R), F32),
                        pltpu.VMEM((2 * cb * FFT_R, 2 * FFT_R), BF),
                        pltpu.VMEM((2 * cb * FFT_R, 2 * FFT_R), F32)],
        compiler_params=pltpu.CompilerParams(dimension_semantics=("parallel", "parallel"),
                                             vmem_limit_bytes=VMEM_LIMIT),
        name="hyena_spectra",
    )(t5, ma, mb, tr, ti)


def hyena_long(ph, lp):
    B, L, W = ph.shape
    assert B == 2 and L == FFT_L
    taps = hyena_taps(L, lp['hy_w1'], lp['hy_b1'], lp['hy_freq'], lp['hy_w2'], lp['hy_b2'],
                      lp['hy_w3'], lp['hy_b3'])
    spectra = hyena_spectra(taps)
    ph3 = ph.reshape(B * L, W).T.reshape(W, FFT_R, FFT_R)
    z3 = hyena_conv(ph3, spectra, lp['hy_short_w'], lp['hy_short_b'], lp['hy_bias'])
    return z3.reshape(HY_WIDTH, B * L)


CONF_TL = 512
CONF_HALO = 16
CONF_ROWS = 32


def _conf_body(a_ref, b_ref, ap_ref, bp_ref, an_ref, bn_ref, w_ref, cb_ref, lg_ref, lb_ref, o_ref, h_scr,
               *, tiles_per_seq):
    tl = a_ref.shape[0]
    i = pl.program_id(0) % tiles_per_seq
    glu = lambda a, b: a * jax.nn.sigmoid(b)
    h_scr[CONF_HALO:CONF_HALO + tl, :] = glu(a_ref[...], b_ref[...])
    h_scr[0:CONF_HALO, :] = jnp.where(i == 0, 0.0, glu(ap_ref[...], bp_ref[...]))
    h_scr[CONF_HALO + tl:2 * CONF_HALO + tl, :] = jnp.where(i == tiles_per_seq - 1, 0.0,
                                                             glu(an_ref[...], bn_ref[...]))
    left = (CONF_TAPS - 1) // 2
    for r0 in range(0, tl, CONF_ROWS):
        acc = jnp.zeros((CONF_ROWS, CONF_WIDTH), F32)
        for k in range(CONF_TAPS):
            start = CONF_HALO - left + r0 + k
            acc = acc + w_ref[k:k + 1, :] * h_scr[start:start + CONF_ROWS, :]
        h = acc + cb_ref[...]
        mu = jnp.mean(h, axis=-1, keepdims=True)
        d = h - mu
        var = jnp.mean(d * d, axis=-1, keepdims=True)
        hn = d * lax.rsqrt(var + EPS) * lg_ref[...] + lb_ref[...]
        o_ref[r0:r0 + CONF_ROWS, :] = (hn * jax.nn.sigmoid(hn)).astype(o_ref.dtype)


def conformer_pre(p2, B, L, col_a, col_b, dw_w, dw_b, ln_g, ln_b):
    W = CONF_WIDTH
    tl = min(CONF_TL, L)
    tps = L // tl
    hb = tl // CONF_HALO
    nhb = B * L // CONF_HALO
    main = lambda col: pl.BlockSpec((tl, W), lambda i: (i, col))
    prev = lambda col: pl.BlockSpec((CONF_HALO, W), lambda i: (jnp.maximum(i * hb - 1, 0), col))
    nxt = lambda col: pl.BlockSpec((CONF_HALO, W), lambda i: (jnp.minimum((i + 1) * hb, nhb - 1), col))
    row = lambda a: a.reshape(1, W).astype(F32)
    vec = pl.BlockSpec((1, W), lambda i: (0, 0))
    return pl.pallas_call(
        functools.partial(_conf_body, tiles_per_seq=tps),
        grid=(B * tps,),
        in_specs=[main(col_a), main(col_b), prev(col_a), prev(col_b), nxt(col_a), nxt(col_b),
                  pl.BlockSpec((CONF_TAPS, W), lambda i: (0, 0)), vec, vec, vec],
        out_specs=pl.BlockSpec((tl, W), lambda i: (i, 0)),
        out_shape=jax.ShapeDtypeStruct((B * L, W), BF),
        scratch_shapes=[pltpu.VMEM((tl + 2 * CONF_HALO, W), F32)],
        compiler_params=pltpu.CompilerParams(dimension_semantics=("parallel",), vmem_limit_bytes=VMEM_LIMIT),
        name="conformer_pre",
    )(p2, p2, p2, p2, p2, p2, dw_w.astype(F32), row(dw_b), row(ln_g), row(ln_b))


MERGE_TM = 512


def _merge_body(x_ref, ret_ref, conf_ref, hy_ref, g0_ref, g1_ref, g2_ref, g3_ref, g4_ref, g5_ref, gate_ref,
                wr_ref, wc_ref, wh_ref, wo_ref, o_ref, *, hy_transposed):
    half = D_MODEL // 2
    b_ret = jnp.dot(ret_ref[...], wr_ref[...], preferred_element_type=F32)
    b_conf = jnp.dot(conf_ref[...], wc_ref[...], preferred_element_type=F32)
    if hy_transposed:
        b_hy = lax.dot_general(hy_ref[...].astype(BF), wh_ref[...], (((0,), (0,)), ((), ())),
                               preferred_element_type=F32)
    else:
        b_hy = jnp.dot(hy_ref[...].astype(BF), wh_ref[...], preferred_element_type=F32)
    sig = jax.nn.sigmoid
    lo = (sig(g0_ref[...]) * b_ret[:, :half] + sig(g2_ref[...]) * b_conf[:, :half]
          + sig(g4_ref[...]) * b_hy[:, :half])
    hi = (sig(g1_ref[...]) * b_ret[:, half:] + sig(g3_ref[...]) * b_conf[:, half:]
          + sig(g5_ref[...]) * b_hy[:, half:])
    m = jnp.concatenate([lo, hi], axis=1).astype(BF)
    y = jnp.dot(m, wo_ref[...], preferred_element_type=F32)
    o_ref[...] = x_ref[...] + gate_ref[0] * y


def merge_branches(x, p2, col_gate, ret_pre, conf_pre, hy, gate, w_ret, w_conf, w_hy, w_out, hy_transposed):
    B, L, D = x.shape
    T = B * L
    tm = min(MERGE_TM, L)
    tps = L // tm
    half = D // 2
    per_batch = gate.shape[0] > 1
    gspec = lambda k: pl.BlockSpec((tm, half), lambda i: (i, col_gate + k))
    full = lambda a: pl.BlockSpec(a.shape, lambda i: (0, 0))
    hy_spec = (pl.BlockSpec((hy.shape[0], tm), lambda i: (0, i)) if hy_transposed
               else pl.BlockSpec((tm, hy.shape[1]), lambda i: (i, 0)))
    ws = [w.astype(BF) for w in (w_ret, w_conf, w_hy, w_out)]
    out = pl.pallas_call(
        functools.partial(_merge_body, hy_transposed=hy_transposed),
        grid=(T // tm,),
        in_specs=[pl.BlockSpec((tm, D), lambda i: (i, 0)),
                  pl.BlockSpec((tm, ret_pre.shape[1]), lambda i: (i, 0)),
                  pl.BlockSpec((tm, conf_pre.shape[1]), lambda i: (i, 0)),
                  hy_spec] + [gspec(k) for k in range(2 * N_BRANCH)]
                 + [pl.BlockSpec((1, 1, D), lambda i: (i // tps if per_batch else 0, 0, 0))]
                 + [full(w) for w in ws],
        out_specs=pl.BlockSpec((tm, D), lambda i: (i, 0)),
        out_shape=jax.ShapeDtypeStruct((T, D), F32),
        compiler_params=pltpu.CompilerParams(dimension_semantics=("parallel",), vmem_limit_bytes=VMEM_LIMIT),
        name="merge_branches",
    )(x.reshape(T, D), ret_pre, conf_pre, hy, p2, p2, p2, p2, p2, p2, gate, *ws)
    return out.reshape(B, L, D)


def mixer_block(x, p, ret_pre, gate, lp):
    B, L, _ = p.shape
    p2 = p.reshape(B * L, -1)
    half = D_MODEL // 2
    conf_pre = conformer_pre(p2, B, L, OFF_CONF // half, OFF_CONF // half + 1,
                             lp['conf_dw_w'], lp['conf_dw_b'], lp['conf_ln_g'], lp['conf_ln_b'])
    long_seq = L == FFT_L
    if long_seq:
        hy = hyena_long(p[..., OFF_HY:OFF_GATE], lp)
    else:
        hy = hyena_branch(p[..., OFF_HY:OFF_GATE], hyena_filters(L, lp), lp).reshape(B * L, HY_WIDTH)
    return merge_branches(x, p2, OFF_GATE // half, ret_pre, conf_pre, hy, gate,
                          lp['w_br_ret'], lp['w_br_conf'], lp['w_br_hy'], lp['w_out'], long_seq)


LOG2E = 1.4426950408889634
NEG_BIG = -1e30
PEER_ROUTE_TM = 512
PEER_TM = 512
PEER_TE = 2048
PEER_ROWS = PEER_TE // PEER_NKEYS
PEER_CHUNK = 256
LANES = 128


def _batcher_pairs(lo, hi):
    def merge(lo, hi, r):
        step = r * 2
        if step < hi - lo:
            yield from merge(lo, hi, step)
            yield from merge(lo + r, hi, step)
            yield from [(i, i + r) for i in range(lo + r, hi - r, step)]
        else:
            yield (lo, lo + r)
    if hi - lo >= 1:
        mid = lo + (hi - lo) // 2
        yield from _batcher_pairs(lo, mid)
        yield from _batcher_pairs(mid + 1, hi)
        yield from merge(lo, hi, 1)


SORT16 = tuple(_batcher_pairs(0, PEER_TOPK - 1))


def _cmpx(v, i, j):
    a, b = v[i], v[j]
    v[i] = jnp.maximum(a, b)
    v[j] = jnp.minimum(a, b)


def _bitonic_desc(v):
    for dist in (8, 4, 2, 1):
        for i in range(PEER_TOPK):
            if not i & dist:
                _cmpx(v, i, i + dist)


def _merge_top(cur, other, dropped):
    out = list(cur)
    for r, o in enumerate(other):
        i = PEER_TOPK - 1 - r
        out[i] = jnp.maximum(cur[i], o)
        dropped = jnp.maximum(dropped, jnp.minimum(cur[i], o))
    _bitonic_desc(out)
    return out, dropped


def _top17_of_rows(s):
    v = [s[8 * k:8 * k + 8, :] for k in range(PEER_NKEYS // 8)]
    for i, j in SORT16:
        _cmpx(v, i, j)
    dropped = jnp.full(v[0].shape, NEG_BIG, jnp.float32)
    for shift in (4, 2, 1):
        other = [pltpu.roll(x, shift, 0) for x in v]
        dropped = jnp.maximum(dropped, pltpu.roll(dropped, shift, 0))
        v, dropped = _merge_top(v, other, dropped)
    return v + [dropped]


def _peer_route_body(hT_ref, wqT_ref, keys_ref, s1p_ref, thr_ref, s2_ref):
    K = PEER_TOPK
    half = PEER_DKEY // 2
    qT = jnp.dot(wqT_ref[...], hT_ref[...], preferred_element_type=jnp.float32)
    s1 = jnp.dot(keys_ref[0], qT[:half], preferred_element_type=jnp.float32,
                 precision=lax.Precision.HIGHEST)
    s2 = jnp.dot(keys_ref[1], qT[half:], preferred_element_type=jnp.float32,
                 precision=lax.Precision.HIGHEST)
    t1 = _top17_of_rows(s1)
    t2 = _top17_of_rows(s2)
    best = [t1[0] + t2[b] for b in range(K)]
    dropped = t1[0] + t2[K]
    for a in range(1, 8):
        best, dropped = _merge_top(best, [t1[a] + t2[b] for b in range((K + 1) // (a + 1))], dropped)
    best, dropped = _merge_top(best, [t1[a] + t2[0] for a in range(8, K + 1)], dropped)
    tau = 0.5 * (best[K - 1] + dropped)
    z = jnp.zeros_like(tau)
    for r in range(K):
        z = z + jnp.exp(best[r] - best[0])
    lse = best[0] + jnp.log(z)
    m2 = t2[0][0:1]
    s1p_ref[0] = jnp.exp2((s1 + (m2 - lse[0:1])) * LOG2E)
    thr_ref[0] = jnp.exp2((tau[0:1] - m2 - s1) * LOG2E)
    s2_ref[0] = jnp.exp2((s2 - m2) * LOG2E)


def peer_route(hT, wqT, keys):
    D, T = hT.shape
    tm = min(PEER_ROUTE_TM, T)
    assert T % tm == 0
    shp = jax.ShapeDtypeStruct((PEER_HEADS, PEER_NKEYS, T), jnp.float32)
    ospec = pl.BlockSpec((1, PEER_NKEYS, tm), lambda t, h: (h, 0, t))
    return pl.pallas_call(
        _peer_route_body,
        grid=(T // tm, PEER_HEADS),
        in_specs=[pl.BlockSpec((D, tm), lambda t, h: (0, t)),
                  pl.BlockSpec((PEER_DKEY, D), lambda t, h: (h, 0)),
                  pl.BlockSpec((2, PEER_NKEYS, PEER_DKEY // 2), lambda t, h: (h, 0, 0))],
        out_specs=[ospec, ospec, ospec],
        out_shape=[shp, shp, shp],
        compiler_params=pltpu.CompilerParams(dimension_semantics=("parallel", "arbitrary")),
        name="peer_route",
    )(hT, wqT, keys.reshape(PEER_HEADS * 2, PEER_NKEYS, PEER_DKEY // 2))


def _gelu_tanh(a):
    inner = a * (0.7978845608028654 + 0.035677408136300125 * (a * a))
    half = 0.5 * a
    return half + half * jnp.tanh(inner)


def _peer_expert_body(u_ref, hT_ref, vT_ref, s1p_ref, thr_ref, s2_ref, x_ref, gate_ref, o_ref,
                      a_scr, w_scr, acc_scr):
    tm = hT_ref.shape[1]

    @pl.when(pl.program_id(1) == 0)
    def _():
        acc_scr[...] = jnp.zeros_like(acc_scr)

    for q in range(PEER_TE // PEER_CHUNK):
        erows = slice(q * PEER_CHUNK, (q + 1) * PEER_CHUNK)
        a_scr[erows, :] = jnp.dot(u_ref[erows, :], hT_ref[...], preferred_element_type=jnp.float32)
    for q in range(PEER_TE // PEER_CHUNK):
        erows = slice(q * PEER_CHUNK, (q + 1) * PEER_CHUNK)
        for r in range(q * PEER_CHUNK // PEER_NKEYS, (q + 1) * PEER_CHUNK // PEER_NKEYS):
            for g in range(tm // LANES):
                lanes = pl.ds(g * LANES, LANES)
                s1b = [jnp.broadcast_to(s1p_ref[h, r:r + 1, lanes], (16, LANES))
                       for h in range(PEER_HEADS)]
                thb = [jnp.broadcast_to(thr_ref[h, r:r + 1, lanes], (16, LANES))
                       for h in range(PEER_HEADS)]
                for jb in range(PEER_NKEYS // 16):
                    rows = pl.ds(r * PEER_NKEYS + jb * 16, 16)
                    gate = jnp.zeros((16, LANES), jnp.float32)
                    for h in range(PEER_HEADS):
                        e2 = s2_ref[h, jb * 16:(jb + 1) * 16, lanes]
                        gate = gate + jnp.where(e2 >= thb[h], e2 * s1b[h], 0.0)
                    w_scr[rows, lanes] = (_gelu_tanh(a_scr[rows, lanes]) * gate).astype(jnp.bfloat16)
        acc_scr[...] += jnp.dot(vT_ref[:, erows], w_scr[erows, :], preferred_element_type=jnp.float32)

    @pl.when(pl.program_id(1) == pl.num_programs(1) - 1)
    def _():
        o_ref[...] = x_ref[...] + gate_ref[0] * acc_scr[...].T


def peer_experts(x2, gate, tiles_per_seq, hT, u_bf, vT_bf, s1p, thr, s2):
    D, T = hT.shape
    E = u_bf.shape[0]
    tm = min(PEER_TM, T)
    te = PEER_TE
    assert T % tm == 0 and E % te == 0
    per_batch = gate.shape[0] > 1
    return pl.pallas_call(
        _peer_expert_body,
        grid=(T // tm, E // te),
        in_specs=[pl.BlockSpec((te, D), lambda t, e: (e, 0)),
                  pl.BlockSpec((D, tm), lambda t, e: (0, t)),
                  pl.BlockSpec((D, te), lambda t, e: (0, e)),
                  pl.BlockSpec((PEER_HEADS, PEER_ROWS, tm), lambda t, e: (0, e, t)),
                  pl.BlockSpec((PEER_HEADS, PEER_ROWS, tm), lambda t, e: (0, e, t)),
                  pl.BlockSpec((PEER_HEADS, PEER_NKEYS, tm), lambda t, e: (0, 0, t)),
                  pl.BlockSpec((tm, D), lambda t, e: (t, 0)),
                  pl.BlockSpec((1, 1, D), lambda t, e: (t // tiles_per_seq if per_batch else 0, 0, 0))],
        out_specs=pl.BlockSpec((tm, D), lambda t, e: (t, 0)),
        out_shape=jax.ShapeDtypeStruct((T, D), jnp.float32),
        scratch_shapes=[pltpu.VMEM((te, tm), jnp.float32),
                        pltpu.VMEM((te, tm), jnp.bfloat16),
                        pltpu.VMEM((D, tm), jnp.float32)],
        compiler_params=pltpu.CompilerParams(dimension_semantics=("parallel", "arbitrary"),
                                             vmem_limit_bytes=48 * 1024 * 1024),
        name="peer_experts",
    )(u_bf, hT, vT_bf, s1p, thr, s2, x2, gate)


def _peer_in_body(x_ref, g_ref, shift_ref, scale_ref, o_ref):
    x = x_ref[...]
    y = x * lax.rsqrt(jnp.mean(x * x, axis=-1, keepdims=True) + EPS) * g_ref[...]
    o_ref[...] = (y * (1.0 + scale_ref[0]) + shift_ref[0]).T.astype(jnp.bfloat16)


def peer_input(x2, gain, shift, scale, tiles_per_seq):
    T, D = x2.shape
    tm = min(PEER_TM, T)
    per_batch = shift.shape[0] > 1
    mod_spec = pl.BlockSpec((1, 1, D), lambda t: (t // tiles_per_seq if per_batch else 0, 0, 0))
    return pl.pallas_call(
        _peer_in_body,
        grid=(T // tm,),
        in_specs=[pl.BlockSpec((tm, D), lambda t: (t, 0)), pl.BlockSpec((1, D), lambda t: (0, 0)),
                  mod_spec, mod_spec],
        out_specs=pl.BlockSpec((D, tm), lambda t: (0, t)),
        out_shape=jax.ShapeDtypeStruct((D, T), jnp.bfloat16),
        compiler_params=pltpu.CompilerParams(dimension_semantics=("parallel",)),
        name="peer_input",
    )(x2, gain.reshape(1, D).astype(jnp.float32), shift, scale)


def peer_ffn(x, gain, shift, scale, gate, w_q, sub_keys, u_bf, vT_bf):
    B, L, D = x.shape
    x2 = x.reshape(B * L, D)
    tiles_per_seq = max(L // min(PEER_TM, B * L), 1)
    hT = peer_input(x2, gain, shift, scale, tiles_per_seq)
    s1p, thr, s2 = peer_route(hT, w_q.T.astype(jnp.bfloat16), sub_keys.astype(jnp.float32))
    return peer_experts(x2, gate, tiles_per_seq, hT, u_bf, vT_bf, s1p, thr, s2).reshape(B, L, D)


def kernel(x, c, ctx, c_ctx, ada_w, ada_b, norm1_g, norm2_g, w_in, ret_decay_logit, ret_gn_g, w_br_ret, conf_dw_w, conf_dw_b, conf_ln_g, conf_ln_b, w_br_conf, hy_short_w, hy_short_b, hy_w1, hy_b1, hy_freq, hy_w2, hy_b2, hy_w3, hy_b3, hy_bias, w_br_hy, w_out, peer_wq, peer_keys, peer_u, peer_v, final_g):
    depth = ada_w.shape[0]
    B, L, _ = x.shape
    Lc = ctx.shape[1]
    HK = RET_HEADS * RET_DK
    HV = RET_HEADS * RET_DV
    ret_cols = dict(col_q=OFF_Q // HK, col_k=OFF_K // HK, col_qp=IN_COLS // HK, col_kp=IN_COLS // HK + 1,
                    col_v=OFF_V // HV)
    cosf, sinf = rope_tables(*grid_rope_angles(L // GRID_W))
    cosf_c, sinf_c = jnp.ones((Lc, HK), F32), jnp.zeros((Lc, HK), F32)
    zero_state = jnp.zeros((B, RET_HEADS, 2 * RET_DK, RET_DV), F32)

    def retention(p, seq, cos_t, sin_t, tabs, s0_f, s0_b, gn_g):
        p2 = p.reshape(B * seq, -1)
        o_f, fin_f = retention_pass(p2, B, seq, cos_t, sin_t, tabs[0], s0_f, backward=False, **ret_cols)
        pre, fin_b = retention_pass(p2, B, seq, cos_t, sin_t, tabs[1], s0_b, backward=True, o_fwd=o_f,
                                    col_g=OFF_G // HV, gn_g=gn_g, **ret_cols)
        return pre, fin_f, fin_b

    for l in range(depth):
        last = l == depth - 1
        lp = {
            'log_g': jax.nn.log_sigmoid(ret_decay_logit[l].astype(jnp.float32)),
            'ret_gn_g': ret_gn_g[l], 'w_br_ret': w_br_ret[l],
            'conf_dw_w': conf_dw_w[l], 'conf_dw_b': conf_dw_b[l],
            'conf_ln_g': conf_ln_g[l], 'conf_ln_b': conf_ln_b[l], 'w_br_conf': w_br_conf[l],
            'hy_short_w': hy_short_w[l], 'hy_short_b': hy_short_b[l],
            'hy_w1': hy_w1[l], 'hy_b1': hy_b1[l], 'hy_freq': hy_freq[l],
            'hy_w2': hy_w2[l], 'hy_b2': hy_b2[l], 'hy_w3': hy_w3[l], 'hy_b3': hy_b3[l],
            'hy_bias': hy_bias[l], 'w_br_hy': w_br_hy[l], 'w_out': w_out[l],
        }
        u_bf = peer_u[l].astype(jnp.bfloat16)
        vT_bf = peer_v[l].T.astype(jnp.bfloat16)
        mod = adaln(c, ada_w[l], ada_b[l])
        mod_c = adaln(c_ctx[None, :], ada_w[l], ada_b[l])

        w_ext = jnp.concatenate([w_in[l], rope_partner_columns(w_in[l][:, OFF_Q:OFF_K]),
                                 rope_partner_columns(w_in[l][:, OFF_K:OFF_V])], axis=1).astype(BF)
        tabs = (retention_tables(lp['log_g'][0], False), retention_tables(lp['log_g'][1], True))

        p_c = norm_project(ctx, norm1_g[l], mod_c[0], mod_c[1], w_ext, INPROJ_COL_BLOCKS)
        pre_c, s_f, s_b = retention(p_c, Lc, cosf_c, sinf_c, tabs, zero_state, zero_state, lp['ret_gn_g'])

        p = norm_project(x, norm1_g[l], mod[0], mod[1], w_ext, INPROJ_COL_BLOCKS)
        pre, _, _ = retention(p, L, cosf, sinf, tabs, s_f, s_b, lp['ret_gn_g'])
        x = mixer_block(x, p, pre, mod[2], lp)

        x = peer_ffn(x, norm2_g[l], mod[3], mod[4], mod[5], peer_wq[l], peer_keys[l], u_bf, vT_bf)

        if not last:
            ctx = mixer_block(ctx, p_c, pre_c, mod_c[2], lp)
            ctx = peer_ffn(ctx, norm2_g[l], mod_c[3], mod_c[4], mod_c[5], peer_wq[l], peer_keys[l], u_bf, vT_bf)
    return rms_norm(x, final_g)
```

```python
import functools
import math

import jax
import jax.numpy as jnp
import numpy as np
from jax import lax
from jax.experimental import pallas as pl
from jax.experimental.pallas import tpu as pltpu

D_MODEL = 1024
GRID_W = 64
EPS = 1e-6

RET_HEADS = 8
RET_DV = D_MODEL // RET_HEADS
RET_DK = RET_DV // 2
RET_CHUNK = 128
ROPE_BASE = 10000.0

CONF_WIDTH = D_MODEL // 2
CONF_TAPS = 31

HY_WIDTH = D_MODEL // 2
HY_ORDER = 2
HY_SHORT_TAPS = 3
HY_BANDS = 16
HY_EMB = 2 * HY_BANDS + 1
HY_HIDDEN = 64
HY_DECAY_SHORT_PCT = 0.3
HY_DECAY_LONG_PCT = 1.5
HY_DECAY_TARGET = 1e-2

PEER_HEADS = 8
PEER_NKEYS = 128
PEER_DKEY = D_MODEL // PEER_HEADS
PEER_TOPK = 16
PEER_TOKEN_BLOCK = 128

N_BRANCH = 3
OFF_Q = 0
OFF_K = OFF_Q + RET_HEADS * RET_DK
OFF_V = OFF_K + RET_HEADS * RET_DK
OFF_G = OFF_V + RET_HEADS * RET_DV
OFF_CONF = OFF_G + RET_HEADS * RET_DV
OFF_HY = OFF_CONF + 2 * CONF_WIDTH
OFF_GATE = OFF_HY + (HY_ORDER + 1) * HY_WIDTH
IN_COLS = OFF_GATE + N_BRANCH * D_MODEL


P_Q = 0
P_K = P_Q + RET_HEADS * RET_DK
P_V = P_K + RET_HEADS * RET_DK
P_G = P_V + RET_HEADS * RET_DV
P_CONF = P_G + RET_HEADS * RET_DV
P_GATE = P_CONF + 2 * CONF_WIDTH
P_QP = P_GATE + N_BRANCH * D_MODEL
P_KP = P_QP + RET_HEADS * RET_DK
P_COLS = P_KP + RET_HEADS * RET_DK
HY_COLS = (HY_ORDER + 1) * HY_WIDTH
INPROJ_TM = 512
INPROJ_COL_BLOCKS = 4


def _inproj_body(x_ref, g_ref, shift_ref, scale_ref, w_ref, wT_ref, o_ref, oT_ref, u_scr):
    @pl.when(pl.program_id(1) == 0)
    def _():
        x = x_ref[...]
        y = x * lax.rsqrt(jnp.mean(x * x, axis=-1, keepdims=True) + EPS) * g_ref[...]
        u = (y * (1.0 + scale_ref[0]) + shift_ref[0]).astype(jnp.bfloat16)
        u_scr[...] = u
        oT_ref[...] = lax.dot_general(wT_ref[...], u, (((1,), (1,)), ((), ())),
                                      preferred_element_type=jnp.float32)

    o_ref[...] = jnp.dot(u_scr[...], w_ref[...], preferred_element_type=jnp.float32)


def norm_project(x, gain, shift, scale, w_bf, wT_bf):
    B, L, D = x.shape
    N = w_bf.shape[1]
    M = wT_bf.shape[0]
    tm = min(INPROJ_TM, L)
    assert L % tm == 0 and N % INPROJ_COL_BLOCKS == 0
    tn = N // INPROJ_COL_BLOCKS
    tiles_per_seq = L // tm
    per_batch = shift.shape[0] > 1
    mod_spec = pl.BlockSpec((1, 1, D), lambda i, j: (i // tiles_per_seq if per_batch else 0, 0, 0))
    out, outT = pl.pallas_call(
        _inproj_body,
        grid=(B * L // tm, INPROJ_COL_BLOCKS),
        in_specs=[pl.BlockSpec((tm, D), lambda i, j: (i, 0)),
                  pl.BlockSpec((1, D), lambda i, j: (0, 0)),
                  mod_spec, mod_spec,
                  pl.BlockSpec((D, tn), lambda i, j: (0, j)),
                  pl.BlockSpec((M, D), lambda i, j: (0, 0))],
        out_specs=[pl.BlockSpec((tm, tn), lambda i, j: (i, j)),
                   pl.BlockSpec((M, tm), lambda i, j: (0, i))],
        out_shape=[jax.ShapeDtypeStruct((B * L, N), jnp.float32),
                   jax.ShapeDtypeStruct((M, B * L), jnp.float32)],
        scratch_shapes=[pltpu.VMEM((tm, D), jnp.bfloat16)],
        compiler_params=pltpu.CompilerParams(dimension_semantics=("parallel", "arbitrary"),
                                             vmem_limit_bytes=VMEM_LIMIT),
        name="norm_project",
    )(x.reshape(B * L, D), gain.reshape(1, D).astype(jnp.float32), shift, scale, w_bf, wT_bf)
    return out.reshape(B, L, N), outT


def rms_norm(x, gain):
    xf = x.astype(jnp.float32)
    y = xf * lax.rsqrt(jnp.mean(xf * xf, axis=-1, keepdims=True) + EPS)
    return (y * gain.astype(jnp.float32)).astype(x.dtype)


def modulate(h, shift, scale):
    return h * (1 + scale) + shift


def adaln(cvec, w, b):
    m = (jax.nn.silu(cvec) @ w + b)[:, None, :]
    return jnp.split(m, 6, axis=-1)


def depthwise_conv(x, w, b):
    taps = w.shape[0]
    left = (taps - 1) // 2
    y = lax.conv_general_dilated(
        x, w[:, None, :].astype(x.dtype), window_strides=(1,),
        padding=[(left, taps - 1 - left)],
        dimension_numbers=('NWC', 'WIO', 'NWC'),
        feature_group_count=x.shape[-1])
    return y + b.astype(x.dtype)


def grid_rope_angles(rows):
    rr, cc = jnp.meshgrid(jnp.arange(rows, dtype=jnp.float32),
                          jnp.arange(GRID_W, dtype=jnp.float32), indexing='ij')
    row = rr.reshape(-1)
    col = cc.reshape(-1)
    n_freq = RET_DK // 4
    inv = ROPE_BASE ** (-jnp.arange(n_freq, dtype=jnp.float32) / n_freq)
    ang = jnp.concatenate([row[:, None] * inv, col[:, None] * inv], axis=-1)
    return jnp.cos(ang), jnp.sin(ang)


def hyena_filters(L, lp):
    f32 = jnp.float32
    t = jnp.arange(L, dtype=f32)
    t_norm = t / max(L - 1, 1)
    w_ang = 2.0 * math.pi * t / L
    bands = jnp.linspace(1e-4, HY_BANDS - 1, HY_BANDS, dtype=f32)
    feat = jnp.concatenate([t_norm[:, None], jnp.cos(w_ang[:, None] * bands),
                            -jnp.sin(w_ang[:, None] * bands)], axis=-1)
    freq = lp['hy_freq'].astype(f32)
    h = jnp.sin(freq * (feat @ lp['hy_w1'].astype(f32) + lp['hy_b1'].astype(f32)))
    h = jnp.sin(freq * (h @ lp['hy_w2'].astype(f32) + lp['hy_b2'].astype(f32)))
    h = (h @ lp['hy_w3'].astype(f32) + lp['hy_b3'].astype(f32)).reshape(L, HY_ORDER, 2, HY_WIDTH)
    deltas = jnp.abs(jnp.linspace(math.log(HY_DECAY_TARGET) / HY_DECAY_LONG_PCT,
                                  math.log(HY_DECAY_TARGET) / HY_DECAY_SHORT_PCT, HY_WIDTH, dtype=f32))
    window = jnp.exp(-t_norm[:, None] * deltas[None, :])
    h = h * window[:, None, None, :]
    h_fwd = h[:, :, 0]
    h_bwd = h[:, :, 1]
    taps = jnp.concatenate([h_fwd, jnp.zeros_like(h_fwd[:1]), jnp.flip(h_bwd[1:], axis=0)], axis=0)
    taps = taps * lax.rsqrt(jnp.sum(taps * taps, axis=0, keepdims=True) + EPS)
    return jnp.fft.rfft(taps, axis=0)


def hyena_branch(ph, filt_fft, lp):
    L = ph.shape[1]
    ph = depthwise_conv(ph, lp['hy_short_w'], lp['hy_short_b'])
    v, x1, x2 = jnp.split(ph, 3, axis=-1)
    z = v.astype(jnp.float32)
    bias = lp['hy_bias'].astype(jnp.float32)
    for o, gate in enumerate((x1, x2)):
        zf = jnp.fft.rfft(z, n=2 * L, axis=1)
        y = jnp.fft.irfft(zf * filt_fft[None, :, o, :], n=2 * L, axis=1)[:, :L]
        z = gate.astype(jnp.float32) * (y + z * bias[o])
    return z


BF = jnp.bfloat16
F32 = jnp.float32
RET_TL = 256
VMEM_LIMIT = 48 * 1024 * 1024


def retention_tables(log_g, backward):
    C = RET_CHUNK
    idx = jnp.arange(C, dtype=F32)
    diff = idx[:, None] - idx[None, :]
    if backward:
        diff = -diff
        qe = C - idx
        ke = idx
    else:
        qe = idx + 1
        ke = C - 1 - idx
    dmask = jnp.where(diff[None] >= 0, jnp.exp(jnp.maximum(diff, 0.0)[None] * log_g[:, None, None]), 0.0)
    qdec = jnp.repeat(jnp.exp(qe[:, None] * log_g[None, :]), RET_DK, axis=1)
    kdec = jnp.repeat(jnp.exp(ke[:, None] * log_g[None, :]), RET_DK, axis=1)
    gc = jnp.broadcast_to(jnp.exp(C * log_g)[:, None, None], (RET_HEADS, 2 * RET_DK, RET_DV))
    return dmask, qdec, kdec, gc


def _ret_body(*refs, backward, nchunk, epilogue):
    if epilogue:
        (q_ref, k_ref, qp_ref, kp_ref, v_ref, cos_ref, sin_ref, dmask_ref, qdec_ref, kdec_ref, gc_ref,
         s0_ref, of_ref, g_ref, gn_ref, o_ref, sfin_ref, s_scr) = refs
    else:
        (q_ref, k_ref, qp_ref, kp_ref, v_ref, cos_ref, sin_ref, dmask_ref, qdec_ref, kdec_ref, gc_ref,
         s0_ref, o_ref, sfin_ref, s_scr) = refs
    C = RET_CHUNK

    @pl.when(pl.program_id(1) == 0)
    def _():
        s_scr[...] = s0_ref[0]

    low = lax.broadcasted_iota(jnp.int32, (C, 2 * RET_DK), 1) < RET_DK
    for ci in range(nchunk):
        c = nchunk - 1 - ci if backward else ci
        rows = slice(c * C, (c + 1) * C)
        cosv = cos_ref[rows, :]
        sinv = sin_ref[rows, :]
        q = q_ref[rows, :] * cosv + qp_ref[rows, :] * sinv
        k = (k_ref[rows, :] * cosv + kp_ref[rows, :] * sinv) * (RET_DK ** -0.5)
        qd = q * qdec_ref[...]
        kd = k * kdec_ref[...]
        for h in range(RET_HEADS):
            grp = slice((h // 2) * 128, (h // 2 + 1) * 128)
            keep = low if h % 2 == 0 else jnp.logical_not(low)
            hv = slice(h * RET_DV, (h + 1) * RET_DV)
            qh = q[:, grp].astype(BF)
            kh = jnp.where(keep, k[:, grp], 0.0).astype(BF)
            kdh = jnp.where(keep, kd[:, grp], 0.0).astype(BF)
            vh = v_ref[rows, hv].astype(BF)
            sc = lax.dot_general(qh, kh, (((1,), (1,)), ((), ())), preferred_element_type=F32)
            sc = sc * dmask_ref[h]
            o = jnp.dot(sc.astype(BF), vh, preferred_element_type=F32)
            o = o + jnp.dot(qd[:, grp].astype(BF), s_scr[h].astype(BF), preferred_element_type=F32)
            kv = lax.dot_general(kdh, vh, (((0,), (0,)), ((), ())), preferred_element_type=F32)
            s_scr[h] = gc_ref[h] * s_scr[h] + kv
            if epilogue:
                o = o + of_ref[rows, hv]
                mu = jnp.mean(o, axis=-1, keepdims=True)
                d = o - mu
                var = jnp.mean(d * d, axis=-1, keepdims=True)
                o = d * lax.rsqrt(var + EPS) * gn_ref[:, hv]
                gg = g_ref[rows, hv]
                o = o * (gg * jax.nn.sigmoid(gg))
            o_ref[rows, hv] = o.astype(o_ref.dtype)

    @pl.when(pl.program_id(1) == pl.num_programs(1) - 1)
    def _():
        sfin_ref[0] = s_scr[...]


def retention_pass(p, B, L, cosf, sinf, tables, s0, *, backward, col_q, col_k, col_qp, col_kp, col_v,
                   o_fwd=None, col_g=None, gn_g=None):
    epilogue = o_fwd is not None
    tl = min(RET_TL, L)
    nt = L // tl
    dmask, qdec, kdec, gc = tables
    HK = RET_HEADS * RET_DK
    HV = RET_HEADS * RET_DV

    def pos(b, n):
        return nt - 1 - n if backward else n

    def tok(b, n):
        return b * nt + pos(b, n)

    def colspec(width, col):
        return pl.BlockSpec((tl, width), lambda b, n: (tok(b, n), col))

    const3 = lambda shape: pl.BlockSpec(shape, lambda b, n: (0, 0, 0))
    const2 = lambda shape: pl.BlockSpec(shape, lambda b, n: (0, 0))
    state = pl.BlockSpec((1, RET_HEADS, 2 * RET_DK, RET_DV), lambda b, n: (b, 0, 0, 0))
    in_specs = [colspec(HK, col_q), colspec(HK, col_k), colspec(HK, col_qp), colspec(HK, col_kp),
                colspec(HV, col_v),
                pl.BlockSpec((tl, HK), lambda b, n: (pos(b, n), 0)),
                pl.BlockSpec((tl, HK), lambda b, n: (pos(b, n), 0)),
                const3((RET_HEADS, RET_CHUNK, RET_CHUNK)), const2((RET_CHUNK, HK)), const2((RET_CHUNK, HK)),
                const3((RET_HEADS, 2 * RET_DK, RET_DV)), state]
    args = [p, p, p, p, p, cosf, sinf, dmask, qdec, kdec, gc, s0]
    if epilogue:
        in_specs += [pl.BlockSpec((tl, HV), lambda b, n: (tok(b, n), 0)), colspec(HV, col_g), const2((1, HV))]
        args += [o_fwd, p, gn_g.reshape(1, HV).astype(F32)]
    return pl.pallas_call(
        functools.partial(_ret_body, backward=backward, nchunk=tl // RET_CHUNK, epilogue=epilogue),
        grid=(B, nt),
        in_specs=in_specs,
        out_specs=[pl.BlockSpec((tl, HV), lambda b, n: (tok(b, n), 0)), state],
        out_shape=[jax.ShapeDtypeStruct((B * L, HV), BF if epilogue else F32),
                   jax.ShapeDtypeStruct((B, RET_HEADS, 2 * RET_DK, RET_DV), F32)],
        scratch_shapes=[pltpu.VMEM((RET_HEADS, 2 * RET_DK, RET_DV), F32)],
        compiler_params=pltpu.CompilerParams(dimension_semantics=("parallel", "arbitrary"),
                                             vmem_limit_bytes=VMEM_LIMIT),
        name="retention_bwd" if backward else "retention_fwd",
    )(*args)


def rope_partner_columns(w):
    w2 = w.reshape(w.shape[0], -1, 2)
    return jnp.stack([-w2[..., 1], w2[..., 0]], axis=-1).reshape(w.shape)


def rope_tables(cos, sin):
    c = jnp.tile(jnp.repeat(cos, 2, axis=1), (1, RET_HEADS))
    s = jnp.tile(jnp.repeat(sin, 2, axis=1), (1, RET_HEADS))
    return c, s


FFT_R = 128
FFT_N = FFT_R * FFT_R
FFT_L = FFT_N // 2
HY_CB = 16
HY_TAPS_CB = 64


def _dft_constants():
    r = np.arange(FFT_R)
    n1 = np.arange(FFT_R // 2)
    fa = np.exp(-2j * np.pi * np.outer(r, n1) / FFT_R)
    ma = np.block([[fa.real, -fa.imag], [fa.imag, fa.real]])
    g = np.exp(-2j * np.pi * np.outer(r, r) / FFT_R)
    mb = np.block([[g.real, g.imag], [-g.imag, g.real]])
    mbi = np.block([[g.real, -g.imag], [g.imag, g.real]])
    p = np.conj(fa).T / FFT_N
    mai = np.block([[p.real, -p.imag], [p.imag, p.real]])
    tw = np.exp(-2j * np.pi * np.outer(r, r) / FFT_N)
    as_bf = lambda a: jnp.asarray(a, F32).astype(BF)
    return (as_bf(ma), as_bf(mb), as_bf(mbi), as_bf(mai),
            jnp.asarray(tw.real, F32), jnp.asarray(tw.imag, F32))


def _chan(c):
    return slice(c * FFT_R, (c + 1) * FFT_R)


def _fft_forward(z_scr, y_scr, ya_scr, x_scr, ma_ref, mb_ref, tr_ref, ti_ref, nch):
    y_scr[...] = jnp.dot(ma_ref[...], z_scr[...].astype(BF), preferred_element_type=F32)
    for c in range(nch):
        ar, ai = y_scr[0:FFT_R, _chan(c)], y_scr[FFT_R:2 * FFT_R, _chan(c)]
        ya_scr[_chan(c), 0:FFT_R] = (ar * tr_ref[...] - ai * ti_ref[...]).astype(BF)
        ya_scr[_chan(c), FFT_R:2 * FFT_R] = (ar * ti_ref[...] + ai * tr_ref[...]).astype(BF)
    x_scr[...] = jnp.dot(ya_scr[...], mb_ref[...], preferred_element_type=F32)


def _hy_conv_body(v_ref, x1_ref, x2_ref, h_ref, sw_ref, sb_ref, bias_ref,
                  ma_ref, mb_ref, mbi_ref, mai_ref, tr_ref, ti_ref, o_ref,
                  z_scr, y_scr, ya_scr, x_scr, xb_scr, b_scr, rhs_scr):
    nch = v_ref.shape[0]
    half = FFT_R // 2
    row = lax.broadcasted_iota(jnp.int32, (FFT_R, FFT_R), 0)
    lane = lax.broadcasted_iota(jnp.int32, (FFT_R, FFT_R), 1)
    first = lane == 0
    last = lane == FFT_R - 1
    seq_start = jnp.logical_and(first, row % half == 0)
    seq_end = jnp.logical_and(last, row % half == half - 1)

    def short_conv(x, s, c):
        a = pltpu.roll(x, 1, 1)
        prev = jnp.where(seq_start, 0.0, jnp.where(first, pltpu.roll(a, 1, 0), a))
        b = pltpu.roll(x, FFT_R - 1, 1)
        nxt = jnp.where(seq_end, 0.0, jnp.where(last, pltpu.roll(b, FFT_R - 1, 0), b))
        return sw_ref[s, 0, c] * prev + sw_ref[s, 1, c] * x + sw_ref[s, 2, c] * nxt + sb_ref[s, c]

    for c in range(nch):
        z_scr[:, _chan(c)] = short_conv(v_ref[c], 0, c)

    for o, gate_ref in enumerate((x1_ref, x2_ref)):
        _fft_forward(z_scr, y_scr, ya_scr, x_scr, ma_ref, mb_ref, tr_ref, ti_ref, nch)
        for c in range(nch):
            xr, xi = x_scr[_chan(c), 0:FFT_R], x_scr[_chan(c), FFT_R:2 * FFT_R]
            hr, hi = h_ref[o, c, :, 0:FFT_R], h_ref[o, c, :, FFT_R:2 * FFT_R]
            xb_scr[_chan(c), 0:FFT_R] = (xr * hr - xi * hi).astype(BF)
            xb_scr[_chan(c), FFT_R:2 * FFT_R] = (xr * hi + xi * hr).astype(BF)
        b_scr[...] = jnp.dot(xb_scr[...], mbi_ref[...], preferred_element_type=F32)
        for c in range(nch):
            br, bi = b_scr[_chan(c), 0:FFT_R], b_scr[_chan(c), FFT_R:2 * FFT_R]
            rhs_scr[0:FFT_R, _chan(c)] = (br * tr_ref[...] + bi * ti_ref[...]).astype(BF)
            rhs_scr[FFT_R:2 * FFT_R, _chan(c)] = (bi * tr_ref[...] - br * ti_ref[...]).astype(BF)
        y_scr[0:FFT_R, :] = jnp.dot(mai_ref[...], rhs_scr[...], preferred_element_type=F32)
        for c in range(nch):
            gate = short_conv(gate_ref[c], o + 1, c)
            z_scr[:, _chan(c)] = gate * (y_scr[0:FFT_R, _chan(c)] + z_scr[:, _chan(c)] * bias_ref[o, c])

    for c in range(nch):
        o_ref[c] = z_scr[:, _chan(c)]


def hyena_conv(ph3, h_filt, short_w, short_b, bias):
    C = HY_WIDTH
    cb = HY_CB
    consts = _dft_constants()
    per_ch = lambda a: a.reshape(a.shape[:-1] + (C, 1, 1)).astype(F32)
    sw = per_ch(short_w.reshape(HY_SHORT_TAPS, HY_ORDER + 1, C).transpose(1, 0, 2))
    sb = per_ch(short_b.reshape(HY_ORDER + 1, C))
    bs = per_ch(bias)
    nb = C // cb
    chan = lambda s: pl.BlockSpec((cb, FFT_R, FFT_R), lambda i: (s * nb + i, 0, 0))
    full = lambda a: pl.BlockSpec(a.shape, lambda i: (0,) * a.ndim)
    return pl.pallas_call(
        _hy_conv_body,
        grid=(nb,),
        in_specs=[chan(0), chan(1), chan(2),
                  pl.BlockSpec((HY_ORDER, cb, FFT_R, 2 * FFT_R), lambda i: (0, i, 0, 0)),
                  pl.BlockSpec((HY_ORDER + 1, HY_SHORT_TAPS, cb, 1, 1), lambda i: (0, 0, i, 0, 0)),
                  pl.BlockSpec((HY_ORDER + 1, cb, 1, 1), lambda i: (0, i, 0, 0)),
                  pl.BlockSpec((HY_ORDER, cb, 1, 1), lambda i: (0, i, 0, 0))]
                 + [full(a) for a in consts],
        out_specs=pl.BlockSpec((cb, FFT_R, FFT_R), lambda i: (i, 0, 0)),
        out_shape=jax.ShapeDtypeStruct((C, FFT_R, FFT_R), F32),
        scratch_shapes=[pltpu.VMEM((FFT_R, cb * FFT_R), F32),
                        pltpu.VMEM((2 * FFT_R, cb * FFT_R), F32),
                        pltpu.VMEM((cb * FFT_R, 2 * FFT_R), BF),
                        pltpu.VMEM((cb * FFT_R, 2 * FFT_R), F32),
                        pltpu.VMEM((cb * FFT_R, 2 * FFT_R), BF),
                        pltpu.VMEM((cb * FFT_R, 2 * FFT_R), F32),
                        pltpu.VMEM((2 * FFT_R, cb * FFT_R), BF)],
        compiler_params=pltpu.CompilerParams(dimension_semantics=("parallel",),
                                             vmem_limit_bytes=VMEM_LIMIT),
        name="hyena_conv",
    )(ph3, ph3, ph3, h_filt, sw, sb, bs, *consts)


def hyena_features(L):
    t = jnp.arange(L, dtype=F32)
    t_norm = t / max(L - 1, 1)
    w_ang = 2.0 * math.pi * t / L
    bands = jnp.linspace(1e-4, HY_BANDS - 1, HY_BANDS, dtype=F32)
    feat = jnp.concatenate([t_norm[:, None], jnp.cos(w_ang[:, None] * bands),
                            -jnp.sin(w_ang[:, None] * bands)], axis=-1)
    return feat.T, t_norm[None, :]


def hyena_deltas():
    return jnp.abs(jnp.linspace(math.log(HY_DECAY_TARGET) / HY_DECAY_LONG_PCT,
                                math.log(HY_DECAY_TARGET) / HY_DECAY_SHORT_PCT, HY_WIDTH, dtype=F32))


def _hy_taps_body(featT_ref, tn_ref, w1T_ref, b1_ref, w2T_ref, b2_ref, freq_ref, w3T_ref, b3_ref, delta_ref,
                  o_ref, h2_scr):
    hi = lax.Precision.HIGHEST

    @pl.when(jnp.logical_and(pl.program_id(0) == 0, pl.program_id(1) == 0))
    def _():
        h1 = jnp.sin(freq_ref[...] * (jnp.dot(w1T_ref[...], featT_ref[...], preferred_element_type=F32,
                                              precision=hi) + b1_ref[...]))
        h2_scr[...] = jnp.sin(freq_ref[...] * (jnp.dot(w2T_ref[...], h1, preferred_element_type=F32,
                                                       precision=hi) + b2_ref[...]))
    window = jnp.exp(-delta_ref[...] * tn_ref[...])
    taps = []
    for d in range(2):
        h3 = jnp.dot(w3T_ref[0, d], h2_scr[...], preferred_element_type=F32, precision=hi) + b3_ref[0, d]
        taps.append(h3 * window)
    lane = lax.broadcasted_iota(jnp.int32, taps[1].shape, 1)
    taps[1] = jnp.where(lane == 0, 0.0, taps[1])
    ssq = jnp.sum(taps[0] * taps[0] + taps[1] * taps[1], axis=1, keepdims=True)
    scale = lax.rsqrt(ssq + EPS)
    o_ref[0, 0] = taps[0] * scale
    o_ref[0, 1] = taps[1] * scale


def hyena_taps(L, w1, b1, freq, w2, b2, w3, b3):
    C = HY_WIDTH
    cb = HY_TAPS_CB
    featT, tn = hyena_features(L)
    col = lambda a: a.reshape(-1, 1).astype(F32)
    w3T = w3.T.reshape(HY_ORDER, 2, C, HY_HIDDEN).astype(F32)
    b3c = b3.reshape(HY_ORDER, 2, C, 1).astype(F32)
    full = lambda a: pl.BlockSpec(a.shape, lambda i, o: (0,) * a.ndim)
    args = [featT, tn, w1.T.astype(F32), col(b1), w2.T.astype(F32), col(b2), col(freq)]
    return pl.pallas_call(
        _hy_taps_body,
        grid=(C // cb, HY_ORDER),
        in_specs=[full(a) for a in args]
                 + [pl.BlockSpec((1, 2, cb, HY_HIDDEN), lambda i, o: (o, 0, i, 0)),
                    pl.BlockSpec((1, 2, cb, 1), lambda i, o: (o, 0, i, 0)),
                    pl.BlockSpec((cb, 1), lambda i, o: (i, 0))],
        out_specs=pl.BlockSpec((1, 2, cb, L), lambda i, o: (o, 0, i, 0)),
        out_shape=jax.ShapeDtypeStruct((HY_ORDER, 2, C, L), F32),
        scratch_shapes=[pltpu.VMEM((HY_HIDDEN, L), F32)],
        compiler_params=pltpu.CompilerParams(dimension_semantics=("arbitrary", "arbitrary"),
                                             vmem_limit_bytes=VMEM_LIMIT),
        name="hyena_taps",
    )(*args, w3T, b3c, col(hyena_deltas()))


def _hy_spec_body(t_ref, ma_ref, mb_ref, tr_ref, ti_ref, o_ref, z_scr, y_scr, ya_scr, x_scr):
    nch = t_ref.shape[2]
    half = FFT_R // 2
    z_scr[half:FFT_R, :] = jnp.zeros((half, 2 * nch * FFT_R), F32)
    for d in range(2):
        for c in range(nch):
            z_scr[0:half, _chan(d * nch + c)] = t_ref[0, d, c]
    _fft_forward(z_scr, y_scr, ya_scr, x_scr, ma_ref, mb_ref, tr_ref, ti_ref, 2 * nch)
    n = nch * FFT_R
    o_ref[0, :, :, 0:FFT_R] = (x_scr[0:n, 0:FFT_R] + x_scr[n:2 * n, 0:FFT_R]).reshape(nch, FFT_R, FFT_R)
    o_ref[0, :, :, FFT_R:2 * FFT_R] = (x_scr[0:n, FFT_R:2 * FFT_R]
                                       - x_scr[n:2 * n, FFT_R:2 * FFT_R]).reshape(nch, FFT_R, FFT_R)


def hyena_spectra(taps):
    C = HY_WIDTH
    cb = HY_CB
    ma, mb, _, _, tr, ti = _dft_constants()
    t5 = taps.reshape(HY_ORDER, 2, C, FFT_R // 2, FFT_R)
    full = lambda a: pl.BlockSpec(a.shape, lambda o, i: (0,) * a.ndim)
    return pl.pallas_call(
        _hy_spec_body,
        grid=(HY_ORDER, C // cb),
        in_specs=[pl.BlockSpec((1, 2, cb, FFT_R // 2, FFT_R), lambda o, i: (o, 0, i, 0, 0))]
                 + [full(a) for a in (ma, mb, tr, ti)],
        out_specs=pl.BlockSpec((1, cb, FFT_R, 2 * FFT_R), lambda o, i: (o, i, 0, 0)),
        out_shape=jax.ShapeDtypeStruct((HY_ORDER, C, FFT_R, 2 * FFT_R), F32),
        scratch_shapes=[pltpu.VMEM((FFT_R, 2 * cb * FFT_R), F32),
                        pltpu.VMEM((2 * FFT_R, 2 * cb * FFT_R), F32),
                        pltpu.VMEM((2 * cb * FFT_R, 2 * FFT_R), BF),
                        pltpu.VMEM((2 * cb * FFT_R, 2 * FFT_R), F32)],
        compiler_params=pltpu.CompilerParams(dimension_semantics=("parallel", "parallel"),
                                             vmem_limit_bytes=VMEM_LIMIT),
        name="hyena_spectra",
    )(t5, ma, mb, tr, ti)


def hyena_long(phT, lp):
    W, T = phT.shape
    assert T == 2 * FFT_L
    taps = hyena_taps(FFT_L, lp['hy_w1'], lp['hy_b1'], lp['hy_freq'], lp['hy_w2'], lp['hy_b2'],
                      lp['hy_w3'], lp['hy_b3'])
    spectra = hyena_spectra(taps)
    z3 = hyena_conv(phT.reshape(W, FFT_R, FFT_R), spectra, lp['hy_short_w'], lp['hy_short_b'], lp['hy_bias'])
    return z3.reshape(HY_WIDTH, T)


CONF_TL = 512
CONF_HALO = 16
CONF_ROWS = 32


def _conf_body(a_ref, b_ref, ap_ref, bp_ref, an_ref, bn_ref, w_ref, cb_ref, lg_ref, lb_ref, o_ref, h_scr,
               *, tiles_per_seq):
    tl = a_ref.shape[0]
    i = pl.program_id(0) % tiles_per_seq
    glu = lambda a, b: a * jax.nn.sigmoid(b)
    h_scr[CONF_HALO:CONF_HALO + tl, :] = glu(a_ref[...], b_ref[...])
    h_scr[0:CONF_HALO, :] = jnp.where(i == 0, 0.0, glu(ap_ref[...], bp_ref[...]))
    h_scr[CONF_HALO + tl:2 * CONF_HALO + tl, :] = jnp.where(i == tiles_per_seq - 1, 0.0,
                                                             glu(an_ref[...], bn_ref[...]))
    left = (CONF_TAPS - 1) // 2
    for r0 in range(0, tl, CONF_ROWS):
        acc = jnp.zeros((CONF_ROWS, CONF_WIDTH), F32)
        for k in range(CONF_TAPS):
            start = CONF_HALO - left + r0 + k
            acc = acc + w_ref[k:k + 1, :] * h_scr[start:start + CONF_ROWS, :]
        h = acc + cb_ref[...]
        mu = jnp.mean(h, axis=-1, keepdims=True)
        d = h - mu
        var = jnp.mean(d * d, axis=-1, keepdims=True)
        hn = d * lax.rsqrt(var + EPS) * lg_ref[...] + lb_ref[...]
        o_ref[r0:r0 + CONF_ROWS, :] = (hn * jax.nn.sigmoid(hn)).astype(o_ref.dtype)


def conformer_pre(p2, B, L, col_a, col_b, dw_w, dw_b, ln_g, ln_b):
    W = CONF_WIDTH
    tl = min(CONF_TL, L)
    tps = L // tl
    hb = tl // CONF_HALO
    nhb = B * L // CONF_HALO
    main = lambda col: pl.BlockSpec((tl, W), lambda i: (i, col))
    prev = lambda col: pl.BlockSpec((CONF_HALO, W), lambda i: (jnp.maximum(i * hb - 1, 0), col))
    nxt = lambda col: pl.BlockSpec((CONF_HALO, W), lambda i: (jnp.minimum((i + 1) * hb, nhb - 1), col))
    row = lambda a: a.reshape(1, W).astype(F32)
    vec = pl.BlockSpec((1, W), lambda i: (0, 0))
    return pl.pallas_call(
        functools.partial(_conf_body, tiles_per_seq=tps),
        grid=(B * tps,),
        in_specs=[main(col_a), main(col_b), prev(col_a), prev(col_b), nxt(col_a), nxt(col_b),
                  pl.BlockSpec((CONF_TAPS, W), lambda i: (0, 0)), vec, vec, vec],
        out_specs=pl.BlockSpec((tl, W), lambda i: (i, 0)),
        out_shape=jax.ShapeDtypeStruct((B * L, W), BF),
        scratch_shapes=[pltpu.VMEM((tl + 2 * CONF_HALO, W), F32)],
        compiler_params=pltpu.CompilerParams(dimension_semantics=("parallel",), vmem_limit_bytes=VMEM_LIMIT),
        name="conformer_pre",
    )(p2, p2, p2, p2, p2, p2, dw_w.astype(F32), row(dw_b), row(ln_g), row(ln_b))


MERGE_TM = 512


def _merge_body(x_ref, ret_ref, conf_ref, hy_ref, g0_ref, g1_ref, g2_ref, g3_ref, g4_ref, g5_ref, gate_ref,
                wr_ref, wc_ref, wh_ref, wo_ref, o_ref, *, hy_transposed):
    half = D_MODEL // 2
    b_ret = jnp.dot(ret_ref[...], wr_ref[...], preferred_element_type=F32)
    b_conf = jnp.dot(conf_ref[...], wc_ref[...], preferred_element_type=F32)
    if hy_transposed:
        b_hy = lax.dot_general(hy_ref[...].astype(BF), wh_ref[...], (((0,), (0,)), ((), ())),
                               preferred_element_type=F32)
    else:
        b_hy = jnp.dot(hy_ref[...].astype(BF), wh_ref[...], preferred_element_type=F32)
    sig = jax.nn.sigmoid
    lo = (sig(g0_ref[...]) * b_ret[:, :half] + sig(g2_ref[...]) * b_conf[:, :half]
          + sig(g4_ref[...]) * b_hy[:, :half])
    hi = (sig(g1_ref[...]) * b_ret[:, half:] + sig(g3_ref[...]) * b_conf[:, half:]
          + sig(g5_ref[...]) * b_hy[:, half:])
    m = jnp.concatenate([lo, hi], axis=1).astype(BF)
    y = jnp.dot(m, wo_ref[...], preferred_element_type=F32)
    o_ref[...] = x_ref[...] + gate_ref[0] * y


def merge_branches(x, p2, col_gate, ret_pre, conf_pre, hy, gate, w_ret, w_conf, w_hy, w_out, hy_transposed):
    B, L, D = x.shape
    T = B * L
    tm = min(MERGE_TM, L)
    tps = L // tm
    half = D // 2
    per_batch = gate.shape[0] > 1
    gspec = lambda k: pl.BlockSpec((tm, half), lambda i: (i, col_gate + k))
    full = lambda a: pl.BlockSpec(a.shape, lambda i: (0, 0))
    hy_spec = (pl.BlockSpec((hy.shape[0], tm), lambda i: (0, i)) if hy_transposed
               else pl.BlockSpec((tm, hy.shape[1]), lambda i: (i, 0)))
    ws = [w.astype(BF) for w in (w_ret, w_conf, w_hy, w_out)]
    out = pl.pallas_call(
        functools.partial(_merge_body, hy_transposed=hy_transposed),
        grid=(T // tm,),
        in_specs=[pl.BlockSpec((tm, D), lambda i: (i, 0)),
                  pl.BlockSpec((tm, ret_pre.shape[1]), lambda i: (i, 0)),
                  pl.BlockSpec((tm, conf_pre.shape[1]), lambda i: (i, 0)),
                  hy_spec] + [gspec(k) for k in range(2 * N_BRANCH)]
                 + [pl.BlockSpec((1, 1, D), lambda i: (i // tps if per_batch else 0, 0, 0))]
                 + [full(w) for w in ws],
        out_specs=pl.BlockSpec((tm, D), lambda i: (i, 0)),
        out_shape=jax.ShapeDtypeStruct((T, D), F32),
        compiler_params=pltpu.CompilerParams(dimension_semantics=("parallel",), vmem_limit_bytes=VMEM_LIMIT),
        name="merge_branches",
    )(x.reshape(T, D), ret_pre, conf_pre, hy, p2, p2, p2, p2, p2, p2, gate, *ws)
    return out.reshape(B, L, D)


def mixer_block(x, p, phT, ret_pre, gate, lp):
    B, L, _ = p.shape
    p2 = p.reshape(B * L, -1)
    half = D_MODEL // 2
    conf_pre = conformer_pre(p2, B, L, P_CONF // half, P_CONF // half + 1,
                             lp['conf_dw_w'], lp['conf_dw_b'], lp['conf_ln_g'], lp['conf_ln_b'])
    long_seq = L == FFT_L
    if long_seq:
        hy = hyena_long(phT, lp)
    else:
        hy = hyena_branch(phT.T.reshape(B, L, HY_COLS), hyena_filters(L, lp), lp).reshape(B * L, HY_WIDTH)
    return merge_branches(x, p2, P_GATE // half, ret_pre, conf_pre, hy, gate,
                          lp['w_br_ret'], lp['w_br_conf'], lp['w_br_hy'], lp['w_out'], long_seq)


LOG2E = 1.4426950408889634
NEG_BIG = -1e30
PEER_ROUTE_TM = 512
PEER_TM = 512
PEER_TE = 2048
PEER_ROWS = PEER_TE // PEER_NKEYS
PEER_CHUNK = 256
LANES = 128


def _batcher_pairs(lo, hi):
    def merge(lo, hi, r):
        step = r * 2
        if step < hi - lo:
            yield from merge(lo, hi, step)
            yield from merge(lo + r, hi, step)
            yield from [(i, i + r) for i in range(lo + r, hi - r, step)]
        else:
            yield (lo, lo + r)
    if hi - lo >= 1:
        mid = lo + (hi - lo) // 2
        yield from _batcher_pairs(lo, mid)
        yield from _batcher_pairs(mid + 1, hi)
        yield from merge(lo, hi, 1)


SORT16 = tuple(_batcher_pairs(0, PEER_TOPK - 1))


def _cmpx(v, i, j):
    a, b = v[i], v[j]
    v[i] = jnp.maximum(a, b)
    v[j] = jnp.minimum(a, b)


def _bitonic_desc(v):
    for dist in (8, 4, 2, 1):
        for i in range(PEER_TOPK):
            if not i & dist:
                _cmpx(v, i, i + dist)


def _merge_top(cur, other, dropped):
    out = list(cur)
    for r, o in enumerate(other):
        i = PEER_TOPK - 1 - r
        out[i] = jnp.maximum(cur[i], o)
        dropped = jnp.maximum(dropped, jnp.minimum(cur[i], o))
    _bitonic_desc(out)
    return out, dropped


def _top17_of_rows(s):
    v = [s[8 * k:8 * k + 8, :] for k in range(PEER_NKEYS // 8)]
    for i, j in SORT16:
        _cmpx(v, i, j)
    dropped = jnp.full(v[0].shape, NEG_BIG, jnp.float32)
    for shift in (4, 2, 1):
        other = [pltpu.roll(x, shift, 0) for x in v]
        dropped = jnp.maximum(dropped, pltpu.roll(dropped, shift, 0))
        v, dropped = _merge_top(v, other, dropped)
    return v + [dropped]


def _peer_route_body(hT_ref, wqT_ref, keys_ref, s1p_ref, thr_ref, s2_ref):
    K = PEER_TOPK
    half = PEER_DKEY // 2
    qT = jnp.dot(wqT_ref[...], hT_ref[...], preferred_element_type=jnp.float32)
    s1 = jnp.dot(keys_ref[0], qT[:half], preferred_element_type=jnp.float32,
                 precision=lax.Precision.HIGHEST)
    s2 = jnp.dot(keys_ref[1], qT[half:], preferred_element_type=jnp.float32,
                 precision=lax.Precision.HIGHEST)
    t1 = _top17_of_rows(s1)
    t2 = _top17_of_rows(s2)
    best = [t1[0] + t2[b] for b in range(K)]
    dropped = t1[0] + t2[K]
    for a in range(1, 8):
        best, dropped = _merge_top(best, [t1[a] + t2[b] for b in range((K + 1) // (a + 1))], dropped)
    best, dropped = _merge_top(best, [t1[a] + t2[0] for a in range(8, K + 1)], dropped)
    tau = 0.5 * (best[K - 1] + dropped)
    z = jnp.zeros_like(tau)
    for r in range(K):
        z = z + jnp.exp(best[r] - best[0])
    lse = best[0] + jnp.log(z)
    m2 = t2[0][0:1]
    s1p_ref[0] = jnp.exp2((s1 + (m2 - lse[0:1])) * LOG2E)
    thr_ref[0] = jnp.exp2((tau[0:1] - m2 - s1) * LOG2E)
    s2_ref[0] = jnp.exp2((s2 - m2) * LOG2E)


def peer_route(hT, wqT, keys):
    D, T = hT.shape
    tm = min(PEER_ROUTE_TM, T)
    assert T % tm == 0
    shp = jax.ShapeDtypeStruct((PEER_HEADS, PEER_NKEYS, T), jnp.float32)
    ospec = pl.BlockSpec((1, PEER_NKEYS, tm), lambda t, h: (h, 0, t))
    return pl.pallas_call(
        _peer_route_body,
        grid=(T // tm, PEER_HEADS),
        in_specs=[pl.BlockSpec((D, tm), lambda t, h: (0, t)),
                  pl.BlockSpec((PEER_DKEY, D), lambda t, h: (h, 0)),
                  pl.BlockSpec((2, PEER_NKEYS, PEER_DKEY // 2), lambda t, h: (h, 0, 0))],
        out_specs=[ospec, ospec, ospec],
        out_shape=[shp, shp, shp],
        compiler_params=pltpu.CompilerParams(dimension_semantics=("parallel", "arbitrary")),
        name="peer_route",
    )(hT, wqT, keys.reshape(PEER_HEADS * 2, PEER_NKEYS, PEER_DKEY // 2))


def _gelu_tanh(a):
    inner = a * (0.7978845608028654 + 0.035677408136300125 * (a * a))
    half = 0.5 * a
    return half + half * jnp.tanh(inner)


def _peer_expert_body(u_ref, hT_ref, v_ref, s1p_ref, thr_ref, s2_ref, x_ref, gate_ref, o_ref,
                      a_scr, w_scr, acc_scr):
    tm = hT_ref.shape[1]

    @pl.when(pl.program_id(1) == 0)
    def _():
        acc_scr[...] = jnp.zeros_like(acc_scr)

    for q in range(PEER_TE // PEER_CHUNK):
        erows = slice(q * PEER_CHUNK, (q + 1) * PEER_CHUNK)
        a_scr[erows, :] = jnp.dot(u_ref[erows, :], hT_ref[...], preferred_element_type=jnp.float32)
    for q in range(PEER_TE // PEER_CHUNK):
        erows = slice(q * PEER_CHUNK, (q + 1) * PEER_CHUNK)
        for r in range(q * PEER_CHUNK // PEER_NKEYS, (q + 1) * PEER_CHUNK // PEER_NKEYS):
            for g in range(tm // LANES):
                lanes = pl.ds(g * LANES, LANES)
                s1b = [jnp.broadcast_to(s1p_ref[h, r:r + 1, lanes], (16, LANES))
                       for h in range(PEER_HEADS)]
                thb = [jnp.broadcast_to(thr_ref[h, r:r + 1, lanes], (16, LANES))
                       for h in range(PEER_HEADS)]
                for jb in range(PEER_NKEYS // 16):
                    rows = pl.ds(r * PEER_NKEYS + jb * 16, 16)
                    gate = jnp.zeros((16, LANES), jnp.float32)
                    for h in range(PEER_HEADS):
                        e2 = s2_ref[h, jb * 16:(jb + 1) * 16, lanes]
                        gate = gate + jnp.where(e2 >= thb[h], e2 * s1b[h], 0.0)
                    w_scr[rows, lanes] = (_gelu_tanh(a_scr[rows, lanes]) * gate).astype(jnp.bfloat16)
        acc_scr[...] += lax.dot_general(w_scr[erows, :], v_ref[erows, :], (((0,), (0,)), ((), ())),
                                        preferred_element_type=jnp.float32)

    @pl.when(pl.program_id(1) == pl.num_programs(1) - 1)
    def _():
        o_ref[...] = x_ref[...] + gate_ref[0] * acc_scr[...]


def peer_experts(x2, gate, tiles_per_seq, hT, u_bf, v_bf, s1p, thr, s2):
    D, T = hT.shape
    E = u_bf.shape[0]
    tm = min(PEER_TM, T)
    te = PEER_TE
    assert T % tm == 0 and E % te == 0
    per_batch = gate.shape[0] > 1
    return pl.pallas_call(
        _peer_expert_body,
        grid=(T // tm, E // te),
        in_specs=[pl.BlockSpec((te, D), lambda t, e: (e, 0)),
                  pl.BlockSpec((D, tm), lambda t, e: (0, t)),
                  pl.BlockSpec((te, D), lambda t, e: (e, 0)),
                  pl.BlockSpec((PEER_HEADS, PEER_ROWS, tm), lambda t, e: (0, e, t)),
                  pl.BlockSpec((PEER_HEADS, PEER_ROWS, tm), lambda t, e: (0, e, t)),
                  pl.BlockSpec((PEER_HEADS, PEER_NKEYS, tm), lambda t, e: (0, 0, t)),
                  pl.BlockSpec((tm, D), lambda t, e: (t, 0)),
                  pl.BlockSpec((1, 1, D), lambda t, e: (t // tiles_per_seq if per_batch else 0, 0, 0))],
        out_specs=pl.BlockSpec((tm, D), lambda t, e: (t, 0)),
        out_shape=jax.ShapeDtypeStruct((T, D), jnp.float32),
        scratch_shapes=[pltpu.VMEM((te, tm), jnp.float32),
                        pltpu.VMEM((te, tm), jnp.bfloat16),
                        pltpu.VMEM((tm, D), jnp.float32)],
        compiler_params=pltpu.CompilerParams(dimension_semantics=("parallel", "arbitrary"),
                                             vmem_limit_bytes=VMEM_LIMIT),
        name="peer_experts",
    )(u_bf, hT, v_bf, s1p, thr, s2, x2, gate)


def _peer_in_body(x_ref, g_ref, shift_ref, scale_ref, o_ref):
    x = x_ref[...]
    y = x * lax.rsqrt(jnp.mean(x * x, axis=-1, keepdims=True) + EPS) * g_ref[...]
    o_ref[...] = (y * (1.0 + scale_ref[0]) + shift_ref[0]).T.astype(jnp.bfloat16)


def peer_input(x2, gain, shift, scale, tiles_per_seq):
    T, D = x2.shape
    tm = min(PEER_TM, T)
    per_batch = shift.shape[0] > 1
    mod_spec = pl.BlockSpec((1, 1, D), lambda t: (t // tiles_per_seq if per_batch else 0, 0, 0))
    return pl.pallas_call(
        _peer_in_body,
        grid=(T // tm,),
        in_specs=[pl.BlockSpec((tm, D), lambda t: (t, 0)), pl.BlockSpec((1, D), lambda t: (0, 0)),
                  mod_spec, mod_spec],
        out_specs=pl.BlockSpec((D, tm), lambda t: (0, t)),
        out_shape=jax.ShapeDtypeStruct((D, T), jnp.bfloat16),
        compiler_params=pltpu.CompilerParams(dimension_semantics=("parallel",)),
        name="peer_input",
    )(x2, gain.reshape(1, D).astype(jnp.float32), shift, scale)


def peer_ffn(x, gain, shift, scale, gate, w_q, sub_keys, u_bf, v_bf):
    B, L, D = x.shape
    x2 = x.reshape(B * L, D)
    tiles_per_seq = max(L // min(PEER_TM, B * L), 1)
    hT = peer_input(x2, gain, shift, scale, tiles_per_seq)
    s1p, thr, s2 = peer_route(hT, w_q.T.astype(jnp.bfloat16), sub_keys.astype(jnp.float32))
    return peer_experts(x2, gate, tiles_per_seq, hT, u_bf, v_bf, s1p, thr, s2).reshape(B, L, D)


def kernel(x, c, ctx, c_ctx, ada_w, ada_b, norm1_g, norm2_g, w_in, ret_decay_logit, ret_gn_g, w_br_ret, conf_dw_w, conf_dw_b, conf_ln_g, conf_ln_b, w_br_conf, hy_short_w, hy_short_b, hy_w1, hy_b1, hy_freq, hy_w2, hy_b2, hy_w3, hy_b3, hy_bias, w_br_hy, w_out, peer_wq, peer_keys, peer_u, peer_v, final_g):
    depth = ada_w.shape[0]
    B, L, _ = x.shape
    Lc = ctx.shape[1]
    HK = RET_HEADS * RET_DK
    HV = RET_HEADS * RET_DV
    ret_cols = dict(col_q=P_Q // HK, col_k=P_K // HK, col_qp=P_QP // HK, col_kp=P_KP // HK, col_v=P_V // HV)
    w_in_bf = w_in.astype(BF)
    cosf, sinf = rope_tables(*grid_rope_angles(L // GRID_W))
    cosf_c, sinf_c = jnp.ones((Lc, HK), F32), jnp.zeros((Lc, HK), F32)
    zero_state = jnp.zeros((B, RET_HEADS, 2 * RET_DK, RET_DV), F32)

    def retention(p, seq, cos_t, sin_t, tabs, s0_f, s0_b, gn_g):
        p2 = p.reshape(B * seq, -1)
        o_f, fin_f = retention_pass(p2, B, seq, cos_t, sin_t, tabs[0], s0_f, backward=False, **ret_cols)
        pre, fin_b = retention_pass(p2, B, seq, cos_t, sin_t, tabs[1], s0_b, backward=True, o_fwd=o_f,
                                    col_g=P_G // HV, gn_g=gn_g, **ret_cols)
        return pre, fin_f, fin_b

    for l in range(depth):
        last = l == depth - 1
        lp = {
            'log_g': jax.nn.log_sigmoid(ret_decay_logit[l].astype(jnp.float32)),
            'ret_gn_g': ret_gn_g[l], 'w_br_ret': w_br_ret[l],
            'conf_dw_w': conf_dw_w[l], 'conf_dw_b': conf_dw_b[l],
            'conf_ln_g': conf_ln_g[l], 'conf_ln_b': conf_ln_b[l], 'w_br_conf': w_br_conf[l],
            'hy_short_w': hy_short_w[l], 'hy_short_b': hy_short_b[l],
            'hy_w1': hy_w1[l], 'hy_b1': hy_b1[l], 'hy_freq': hy_freq[l],
            'hy_w2': hy_w2[l], 'hy_b2': hy_b2[l], 'hy_w3': hy_w3[l], 'hy_b3': hy_b3[l],
            'hy_bias': hy_bias[l], 'w_br_hy': w_br_hy[l], 'w_out': w_out[l],
        }
        u_bf = peer_u[l].astype(jnp.bfloat16)
        v_bf = peer_v[l].astype(jnp.bfloat16)
        mod = adaln(c, ada_w[l], ada_b[l])
        mod_c = adaln(c_ctx[None, :], ada_w[l], ada_b[l])

        wl = w_in_bf[l]
        w_tok = jnp.concatenate([wl[:, :OFF_HY], wl[:, OFF_GATE:], rope_partner_columns(wl[:, OFF_Q:OFF_K]),
                                 rope_partner_columns(wl[:, OFF_K:OFF_V])], axis=1)
        w_hyT = wl[:, OFF_HY:OFF_GATE].T
        tabs = (retention_tables(lp['log_g'][0], False), retention_tables(lp['log_g'][1], True))

        p_c, phT_c = norm_project(ctx, norm1_g[l], mod_c[0], mod_c[1], w_tok, w_hyT)
        pre_c, s_f, s_b = retention(p_c, Lc, cosf_c, sinf_c, tabs, zero_state, zero_state, lp['ret_gn_g'])

        p, phT = norm_project(x, norm1_g[l], mod[0], mod[1], w_tok, w_hyT)
        pre, _, _ = retention(p, L, cosf, sinf, tabs, s_f, s_b, lp['ret_gn_g'])
        x = mixer_block(x, p, phT, pre, mod[2], lp)

        x = peer_ffn(x, norm2_g[l], mod[3], mod[4], mod[5], peer_wq[l], peer_keys[l], u_bf, v_bf)

        if not last:
            ctx = mixer_block(ctx, p_c, phT_c, pre_c, mod_c[2], lp)
            ctx = peer_ffn(ctx, norm2_g[l], mod_c[3], mod_c[4], mod_c[5], peer_wq[l], peer_keys[l], u_bf, v_bf)
    return rms_norm(x, final_g)
```

```python
import functools
import math

import jax
import jax.numpy as jnp
import numpy as np
from jax import lax
from jax.experimental import pallas as pl
from jax.experimental.pallas import tpu as pltpu

D_MODEL = 1024
GRID_W = 64
EPS = 1e-6

RET_HEADS = 8
RET_DV = D_MODEL // RET_HEADS
RET_DK = RET_DV // 2
RET_CHUNK = 128
ROPE_BASE = 10000.0

CONF_WIDTH = D_MODEL // 2
CONF_TAPS = 31

HY_WIDTH = D_MODEL // 2
HY_ORDER = 2
HY_SHORT_TAPS = 3
HY_BANDS = 16
HY_EMB = 2 * HY_BANDS + 1
HY_HIDDEN = 64
HY_DECAY_SHORT_PCT = 0.3
HY_DECAY_LONG_PCT = 1.5
HY_DECAY_TARGET = 1e-2

PEER_HEADS = 8
PEER_NKEYS = 128
PEER_DKEY = D_MODEL // PEER_HEADS
PEER_TOPK = 16
PEER_TOKEN_BLOCK = 128

N_BRANCH = 3
OFF_Q = 0
OFF_K = OFF_Q + RET_HEADS * RET_DK
OFF_V = OFF_K + RET_HEADS * RET_DK
OFF_G = OFF_V + RET_HEADS * RET_DV
OFF_CONF = OFF_G + RET_HEADS * RET_DV
OFF_HY = OFF_CONF + 2 * CONF_WIDTH
OFF_GATE = OFF_HY + (HY_ORDER + 1) * HY_WIDTH
IN_COLS = OFF_GATE + N_BRANCH * D_MODEL


P_Q = 0
P_K = P_Q + RET_HEADS * RET_DK
P_V = P_K + RET_HEADS * RET_DK
P_G = P_V + RET_HEADS * RET_DV
P_CONF = P_G + RET_HEADS * RET_DV
P_GATE = P_CONF + 2 * CONF_WIDTH
P_QP = P_GATE + N_BRANCH * D_MODEL
P_KP = P_QP + RET_HEADS * RET_DK
P_COLS = P_KP + RET_HEADS * RET_DK
HY_COLS = (HY_ORDER + 1) * HY_WIDTH
INPROJ_TM = 512
INPROJ_COL_BLOCKS = 4


def _inproj_body(x_ref, g_ref, shift_ref, scale_ref, w_ref, wT_ref, o_ref, oT_ref, u_scr):
    @pl.when(pl.program_id(1) == 0)
    def _():
        x = x_ref[...]
        y = x * lax.rsqrt(jnp.mean(x * x, axis=-1, keepdims=True) + EPS) * g_ref[...]
        u = (y * (1.0 + scale_ref[0]) + shift_ref[0]).astype(jnp.bfloat16)
        u_scr[...] = u
        oT_ref[...] = lax.dot_general(wT_ref[...], u, (((1,), (1,)), ((), ())),
                                      preferred_element_type=jnp.float32)

    o_ref[...] = jnp.dot(u_scr[...], w_ref[...], preferred_element_type=jnp.float32)


def norm_project(x, gain, shift, scale, w_bf, wT_bf):
    B, L, D = x.shape
    N = w_bf.shape[1]
    M = wT_bf.shape[0]
    tm = min(INPROJ_TM, L)
    assert L % tm == 0 and N % INPROJ_COL_BLOCKS == 0
    tn = N // INPROJ_COL_BLOCKS
    tiles_per_seq = L // tm
    per_batch = shift.shape[0] > 1
    mod_spec = pl.BlockSpec((1, 1, D), lambda i, j: (i // tiles_per_seq if per_batch else 0, 0, 0))
    out, outT = pl.pallas_call(
        _inproj_body,
        grid=(B * L // tm, INPROJ_COL_BLOCKS),
        in_specs=[pl.BlockSpec((tm, D), lambda i, j: (i, 0)),
                  pl.BlockSpec((1, D), lambda i, j: (0, 0)),
                  mod_spec, mod_spec,
                  pl.BlockSpec((D, tn), lambda i, j: (0, j)),
                  pl.BlockSpec((M, D), lambda i, j: (0, 0))],
        out_specs=[pl.BlockSpec((tm, tn), lambda i, j: (i, j)),
                   pl.BlockSpec((M, tm), lambda i, j: (0, i))],
        out_shape=[jax.ShapeDtypeStruct((B * L, N), jnp.float32),
                   jax.ShapeDtypeStruct((M, B * L), jnp.float32)],
        scratch_shapes=[pltpu.VMEM((tm, D), jnp.bfloat16)],
        compiler_params=pltpu.CompilerParams(dimension_semantics=("parallel", "arbitrary"),
                                             vmem_limit_bytes=VMEM_LIMIT),
        name="norm_project",
    )(x.reshape(B * L, D), gain.reshape(1, D).astype(jnp.float32), shift, scale, w_bf, wT_bf)
    return out.reshape(B, L, N), outT


def rms_norm(x, gain):
    xf = x.astype(jnp.float32)
    y = xf * lax.rsqrt(jnp.mean(xf * xf, axis=-1, keepdims=True) + EPS)
    return (y * gain.astype(jnp.float32)).astype(x.dtype)


def modulate(h, shift, scale):
    return h * (1 + scale) + shift


def adaln(cvec, w, b):
    m = (jax.nn.silu(cvec) @ w + b)[:, None, :]
    return jnp.split(m, 6, axis=-1)


def depthwise_conv(x, w, b):
    taps = w.shape[0]
    left = (taps - 1) // 2
    y = lax.conv_general_dilated(
        x, w[:, None, :].astype(x.dtype), window_strides=(1,),
        padding=[(left, taps - 1 - left)],
        dimension_numbers=('NWC', 'WIO', 'NWC'),
        feature_group_count=x.shape[-1])
    return y + b.astype(x.dtype)


def grid_rope_angles(rows):
    rr, cc = jnp.meshgrid(jnp.arange(rows, dtype=jnp.float32),
                          jnp.arange(GRID_W, dtype=jnp.float32), indexing='ij')
    row = rr.reshape(-1)
    col = cc.reshape(-1)
    n_freq = RET_DK // 4
    inv = ROPE_BASE ** (-jnp.arange(n_freq, dtype=jnp.float32) / n_freq)
    ang = jnp.concatenate([row[:, None] * inv, col[:, None] * inv], axis=-1)
    return jnp.cos(ang), jnp.sin(ang)


def hyena_filters(L, lp):
    f32 = jnp.float32
    t = jnp.arange(L, dtype=f32)
    t_norm = t / max(L - 1, 1)
    w_ang = 2.0 * math.pi * t / L
    bands = jnp.linspace(1e-4, HY_BANDS - 1, HY_BANDS, dtype=f32)
    feat = jnp.concatenate([t_norm[:, None], jnp.cos(w_ang[:, None] * bands),
                            -jnp.sin(w_ang[:, None] * bands)], axis=-1)
    freq = lp['hy_freq'].astype(f32)
    h = jnp.sin(freq * (feat @ lp['hy_w1'].astype(f32) + lp['hy_b1'].astype(f32)))
    h = jnp.sin(freq * (h @ lp['hy_w2'].astype(f32) + lp['hy_b2'].astype(f32)))
    h = (h @ lp['hy_w3'].astype(f32) + lp['hy_b3'].astype(f32)).reshape(L, HY_ORDER, 2, HY_WIDTH)
    deltas = jnp.abs(jnp.linspace(math.log(HY_DECAY_TARGET) / HY_DECAY_LONG_PCT,
                                  math.log(HY_DECAY_TARGET) / HY_DECAY_SHORT_PCT, HY_WIDTH, dtype=f32))
    window = jnp.exp(-t_norm[:, None] * deltas[None, :])
    h = h * window[:, None, None, :]
    h_fwd = h[:, :, 0]
    h_bwd = h[:, :, 1]
    taps = jnp.concatenate([h_fwd, jnp.zeros_like(h_fwd[:1]), jnp.flip(h_bwd[1:], axis=0)], axis=0)
    taps = taps * lax.rsqrt(jnp.sum(taps * taps, axis=0, keepdims=True) + EPS)
    return jnp.fft.rfft(taps, axis=0)


def hyena_branch(ph, filt_fft, lp):
    L = ph.shape[1]
    ph = depthwise_conv(ph, lp['hy_short_w'], lp['hy_short_b'])
    v, x1, x2 = jnp.split(ph, 3, axis=-1)
    z = v.astype(jnp.float32)
    bias = lp['hy_bias'].astype(jnp.float32)
    for o, gate in enumerate((x1, x2)):
        zf = jnp.fft.rfft(z, n=2 * L, axis=1)
        y = jnp.fft.irfft(zf * filt_fft[None, :, o, :], n=2 * L, axis=1)[:, :L]
        z = gate.astype(jnp.float32) * (y + z * bias[o])
    return z


BF = jnp.bfloat16
F32 = jnp.float32
RET_TL = 256
VMEM_LIMIT = 48 * 1024 * 1024


def retention_tables(log_g, backward):
    C = RET_CHUNK
    idx = jnp.arange(C, dtype=F32)
    diff = idx[:, None] - idx[None, :]
    if backward:
        diff = -diff
        qe = C - idx
        ke = idx
    else:
        qe = idx + 1
        ke = C - 1 - idx
    dmask = jnp.where(diff[None] >= 0, jnp.exp(jnp.maximum(diff, 0.0)[None] * log_g[:, None, None]), 0.0)
    qdec = jnp.repeat(jnp.exp(qe[:, None] * log_g[None, :]), RET_DK, axis=1)
    kdec = jnp.repeat(jnp.exp(ke[:, None] * log_g[None, :]), RET_DK, axis=1)
    gc = jnp.broadcast_to(jnp.exp(C * log_g)[:, None, None], (RET_HEADS, 2 * RET_DK, RET_DV))
    return dmask, qdec, kdec, gc


def _ret_body(*refs, backward, nchunk, epilogue):
    if epilogue:
        (q_ref, k_ref, qp_ref, kp_ref, v_ref, cos_ref, sin_ref, dmask_ref, qdec_ref, kdec_ref, gc_ref,
         s0_ref, of_ref, g_ref, gn_ref, o_ref, sfin_ref, s_scr) = refs
    else:
        (q_ref, k_ref, qp_ref, kp_ref, v_ref, cos_ref, sin_ref, dmask_ref, qdec_ref, kdec_ref, gc_ref,
         s0_ref, o_ref, sfin_ref, s_scr) = refs
    C = RET_CHUNK

    @pl.when(pl.program_id(1) == 0)
    def _():
        s_scr[...] = s0_ref[0]

    low = lax.broadcasted_iota(jnp.int32, (C, 2 * RET_DK), 1) < RET_DK
    for ci in range(nchunk):
        c = nchunk - 1 - ci if backward else ci
        rows = slice(c * C, (c + 1) * C)
        cosv = cos_ref[rows, :]
        sinv = sin_ref[rows, :]
        q = q_ref[rows, :] * cosv + qp_ref[rows, :] * sinv
        k = (k_ref[rows, :] * cosv + kp_ref[rows, :] * sinv) * (RET_DK ** -0.5)
        qd = q * qdec_ref[...]
        kd = k * kdec_ref[...]
        for h in range(RET_HEADS):
            grp = slice((h // 2) * 128, (h // 2 + 1) * 128)
            keep = low if h % 2 == 0 else jnp.logical_not(low)
            hv = slice(h * RET_DV, (h + 1) * RET_DV)
            qh = q[:, grp].astype(BF)
            kh = jnp.where(keep, k[:, grp], 0.0).astype(BF)
            kdh = jnp.where(keep, kd[:, grp], 0.0).astype(BF)
            vh = v_ref[rows, hv].astype(BF)
            sc = lax.dot_general(qh, kh, (((1,), (1,)), ((), ())), preferred_element_type=F32)
            sc = sc * dmask_ref[h]
            o = jnp.dot(sc.astype(BF), vh, preferred_element_type=F32)
            o = o + jnp.dot(qd[:, grp].astype(BF), s_scr[h].astype(BF), preferred_element_type=F32)
            kv = lax.dot_general(kdh, vh, (((0,), (0,)), ((), ())), preferred_element_type=F32)
            s_scr[h] = gc_ref[h] * s_scr[h] + kv
            if epilogue:
                o = o + of_ref[rows, hv]
                mu = jnp.mean(o, axis=-1, keepdims=True)
                d = o - mu
                var = jnp.mean(d * d, axis=-1, keepdims=True)
                o = d * lax.rsqrt(var + EPS) * gn_ref[:, hv]
                gg = g_ref[rows, hv]
                o = o * (gg * jax.nn.sigmoid(gg))
            o_ref[rows, hv] = o.astype(o_ref.dtype)

    @pl.when(pl.program_id(1) == pl.num_programs(1) - 1)
    def _():
        sfin_ref[0] = s_scr[...]


def retention_pass(p, B, L, cosf, sinf, tables, s0, *, backward, col_q, col_k, col_qp, col_kp, col_v,
                   o_fwd=None, col_g=None, gn_g=None):
    epilogue = o_fwd is not None
    tl = min(RET_TL, L)
    nt = L // tl
    dmask, qdec, kdec, gc = tables
    HK = RET_HEADS * RET_DK
    HV = RET_HEADS * RET_DV

    def pos(b, n):
        return nt - 1 - n if backward else n

    def tok(b, n):
        return b * nt + pos(b, n)

    def colspec(width, col):
        return pl.BlockSpec((tl, width), lambda b, n: (tok(b, n), col))

    const3 = lambda shape: pl.BlockSpec(shape, lambda b, n: (0, 0, 0))
    const2 = lambda shape: pl.BlockSpec(shape, lambda b, n: (0, 0))
    state = pl.BlockSpec((1, RET_HEADS, 2 * RET_DK, RET_DV), lambda b, n: (b, 0, 0, 0))
    in_specs = [colspec(HK, col_q), colspec(HK, col_k), colspec(HK, col_qp), colspec(HK, col_kp),
                colspec(HV, col_v),
                pl.BlockSpec((tl, HK), lambda b, n: (pos(b, n), 0)),
                pl.BlockSpec((tl, HK), lambda b, n: (pos(b, n), 0)),
                const3((RET_HEADS, RET_CHUNK, RET_CHUNK)), const2((RET_CHUNK, HK)), const2((RET_CHUNK, HK)),
                const3((RET_HEADS, 2 * RET_DK, RET_DV)), state]
    args = [p, p, p, p, p, cosf, sinf, dmask, qdec, kdec, gc, s0]
    if epilogue:
        in_specs += [pl.BlockSpec((tl, HV), lambda b, n: (tok(b, n), 0)), colspec(HV, col_g), const2((1, HV))]
        args += [o_fwd, p, gn_g.reshape(1, HV).astype(F32)]
    return pl.pallas_call(
        functools.partial(_ret_body, backward=backward, nchunk=tl // RET_CHUNK, epilogue=epilogue),
        grid=(B, nt),
        in_specs=in_specs,
        out_specs=[pl.BlockSpec((tl, HV), lambda b, n: (tok(b, n), 0)), state],
        out_shape=[jax.ShapeDtypeStruct((B * L, HV), BF if epilogue else F32),
                   jax.ShapeDtypeStruct((B, RET_HEADS, 2 * RET_DK, RET_DV), F32)],
        scratch_shapes=[pltpu.VMEM((RET_HEADS, 2 * RET_DK, RET_DV), F32)],
        compiler_params=pltpu.CompilerParams(dimension_semantics=("parallel", "arbitrary"),
                                             vmem_limit_bytes=VMEM_LIMIT),
        name="retention_bwd" if backward else "retention_fwd",
    )(*args)


def rope_partner_columns(w):
    w2 = w.reshape(w.shape[0], -1, 2)
    return jnp.stack([-w2[..., 1], w2[..., 0]], axis=-1).reshape(w.shape)


def rope_tables(cos, sin):
    c = jnp.tile(jnp.repeat(cos, 2, axis=1), (1, RET_HEADS))
    s = jnp.tile(jnp.repeat(sin, 2, axis=1), (1, RET_HEADS))
    return c, s


FFT_R = 128
FFT_N = FFT_R * FFT_R
FFT_L = FFT_N // 2
HY_CB = 16
HY_TAPS_CB = 64


def _dft_constants():
    r = np.arange(FFT_R)
    n1 = np.arange(FFT_R // 2)
    fa = np.exp(-2j * np.pi * np.outer(r, n1) / FFT_R)
    ma = np.block([[fa.real, -fa.imag], [fa.imag, fa.real]])
    g = np.exp(-2j * np.pi * np.outer(r, r) / FFT_R)
    mb = np.block([[g.real, g.imag], [-g.imag, g.real]])
    mbi = np.block([[g.real, -g.imag], [g.imag, g.real]])
    p = np.conj(fa).T / FFT_N
    mai = np.block([[p.real, -p.imag], [p.imag, p.real]])
    tw = np.exp(-2j * np.pi * np.outer(r, r) / FFT_N)
    as_bf = lambda a: jnp.asarray(a, F32).astype(BF)
    return (as_bf(ma), as_bf(mb), as_bf(mbi), as_bf(mai),
            jnp.asarray(tw.real, F32), jnp.asarray(tw.imag, F32))


def _chan(c):
    return slice(c * FFT_R, (c + 1) * FFT_R)


def _fft_forward(z_scr, y_scr, ya_scr, x_scr, ma_ref, mb_ref, tr_ref, ti_ref, nch):
    y_scr[...] = jnp.dot(ma_ref[...], z_scr[...].astype(BF), preferred_element_type=F32)
    for c in range(nch):
        ar, ai = y_scr[0:FFT_R, _chan(c)], y_scr[FFT_R:2 * FFT_R, _chan(c)]
        ya_scr[_chan(c), 0:FFT_R] = (ar * tr_ref[...] - ai * ti_ref[...]).astype(BF)
        ya_scr[_chan(c), FFT_R:2 * FFT_R] = (ar * ti_ref[...] + ai * tr_ref[...]).astype(BF)
    x_scr[...] = jnp.dot(ya_scr[...], mb_ref[...], preferred_element_type=F32)


def _hy_conv_body(v_ref, x1_ref, x2_ref, h_ref, sw_ref, sb_ref, bias_ref,
                  ma_ref, mb_ref, mbi_ref, mai_ref, tr_ref, ti_ref, o_ref,
                  z_scr, y_scr, ya_scr, x_scr, xb_scr, b_scr, rhs_scr):
    nch = v_ref.shape[0]
    half = FFT_R // 2
    row = lax.broadcasted_iota(jnp.int32, (FFT_R, FFT_R), 0)
    lane = lax.broadcasted_iota(jnp.int32, (FFT_R, FFT_R), 1)
    first = lane == 0
    last = lane == FFT_R - 1
    seq_start = jnp.logical_and(first, row % half == 0)
    seq_end = jnp.logical_and(last, row % half == half - 1)

    def short_conv(x, s, c):
        a = pltpu.roll(x, 1, 1)
        prev = jnp.where(seq_start, 0.0, jnp.where(first, pltpu.roll(a, 1, 0), a))
        b = pltpu.roll(x, FFT_R - 1, 1)
        nxt = jnp.where(seq_end, 0.0, jnp.where(last, pltpu.roll(b, FFT_R - 1, 0), b))
        return sw_ref[s, 0, c] * prev + sw_ref[s, 1, c] * x + sw_ref[s, 2, c] * nxt + sb_ref[s, c]

    for c in range(nch):
        z_scr[:, _chan(c)] = short_conv(v_ref[c], 0, c)

    for o, gate_ref in enumerate((x1_ref, x2_ref)):
        _fft_forward(z_scr, y_scr, ya_scr, x_scr, ma_ref, mb_ref, tr_ref, ti_ref, nch)
        for c in range(nch):
            xr, xi = x_scr[_chan(c), 0:FFT_R], x_scr[_chan(c), FFT_R:2 * FFT_R]
            hr, hi = h_ref[o, c, :, 0:FFT_R], h_ref[o, c, :, FFT_R:2 * FFT_R]
            xb_scr[_chan(c), 0:FFT_R] = (xr * hr - xi * hi).astype(BF)
            xb_scr[_chan(c), FFT_R:2 * FFT_R] = (xr * hi + xi * hr).astype(BF)
        b_scr[...] = jnp.dot(xb_scr[...], mbi_ref[...], preferred_element_type=F32)
        for c in range(nch):
            br, bi = b_scr[_chan(c), 0:FFT_R], b_scr[_chan(c), FFT_R:2 * FFT_R]
            rhs_scr[0:FFT_R, _chan(c)] = (br * tr_ref[...] + bi * ti_ref[...]).astype(BF)
            rhs_scr[FFT_R:2 * FFT_R, _chan(c)] = (bi * tr_ref[...] - br * ti_ref[...]).astype(BF)
        y_scr[0:FFT_R, :] = jnp.dot(mai_ref[...], rhs_scr[...], preferred_element_type=F32)
        for c in range(nch):
            gate = short_conv(gate_ref[c], o + 1, c)
            z_scr[:, _chan(c)] = gate * (y_scr[0:FFT_R, _chan(c)] + z_scr[:, _chan(c)] * bias_ref[o, c])

    for c in range(nch):
        o_ref[c] = z_scr[:, _chan(c)]


def hyena_conv(ph3, h_filt, short_w, short_b, bias):
    C = HY_WIDTH
    cb = HY_CB
    consts = _dft_constants()
    per_ch = lambda a: a.reshape(a.shape[:-1] + (C, 1, 1)).astype(F32)
    sw = per_ch(short_w.reshape(HY_SHORT_TAPS, HY_ORDER + 1, C).transpose(1, 0, 2))
    sb = per_ch(short_b.reshape(HY_ORDER + 1, C))
    bs = per_ch(bias)
    nb = C // cb
    chan = lambda s: pl.BlockSpec((cb, FFT_R, FFT_R), lambda i: (s * nb + i, 0, 0))
    full = lambda a: pl.BlockSpec(a.shape, lambda i: (0,) * a.ndim)
    return pl.pallas_call(
        _hy_conv_body,
        grid=(nb,),
        in_specs=[chan(0), chan(1), chan(2),
                  pl.BlockSpec((HY_ORDER, cb, FFT_R, 2 * FFT_R), lambda i: (0, i, 0, 0)),
                  pl.BlockSpec((HY_ORDER + 1, HY_SHORT_TAPS, cb, 1, 1), lambda i: (0, 0, i, 0, 0)),
                  pl.BlockSpec((HY_ORDER + 1, cb, 1, 1), lambda i: (0, i, 0, 0)),
                  pl.BlockSpec((HY_ORDER, cb, 1, 1), lambda i: (0, i, 0, 0))]
                 + [full(a) for a in consts],
        out_specs=pl.BlockSpec((cb, FFT_R, FFT_R), lambda i: (i, 0, 0)),
        out_shape=jax.ShapeDtypeStruct((C, FFT_R, FFT_R), F32),
        scratch_shapes=[pltpu.VMEM((FFT_R, cb * FFT_R), F32),
                        pltpu.VMEM((2 * FFT_R, cb * FFT_R), F32),
                        pltpu.VMEM((cb * FFT_R, 2 * FFT_R), BF),
                        pltpu.VMEM((cb * FFT_R, 2 * FFT_R), F32),
                        pltpu.VMEM((cb * FFT_R, 2 * FFT_R), BF),
                        pltpu.VMEM((cb * FFT_R, 2 * FFT_R), F32),
                        pltpu.VMEM((2 * FFT_R, cb * FFT_R), BF)],
        compiler_params=pltpu.CompilerParams(dimension_semantics=("parallel",),
                                             vmem_limit_bytes=VMEM_LIMIT),
        name="hyena_conv",
    )(ph3, ph3, ph3, h_filt, sw, sb, bs, *consts)


def hyena_features(L):
    t = jnp.arange(L, dtype=F32)
    t_norm = t / max(L - 1, 1)
    w_ang = 2.0 * math.pi * t / L
    bands = jnp.linspace(1e-4, HY_BANDS - 1, HY_BANDS, dtype=F32)
    feat = jnp.concatenate([t_norm[:, None], jnp.cos(w_ang[:, None] * bands),
                            -jnp.sin(w_ang[:, None] * bands)], axis=-1)
    return feat.T, t_norm[None, :]


def hyena_deltas():
    return jnp.abs(jnp.linspace(math.log(HY_DECAY_TARGET) / HY_DECAY_LONG_PCT,
                                math.log(HY_DECAY_TARGET) / HY_DECAY_SHORT_PCT, HY_WIDTH, dtype=F32))


def _hy_taps_body(featT_ref, tn_ref, w1T_ref, b1_ref, w2T_ref, b2_ref, freq_ref, w3T_ref, b3_ref, delta_ref,
                  o_ref, h2_scr):
    hi = lax.Precision.HIGHEST

    @pl.when(jnp.logical_and(pl.program_id(0) == 0, pl.program_id(1) == 0))
    def _():
        h1 = jnp.sin(freq_ref[...] * (jnp.dot(w1T_ref[...], featT_ref[...], preferred_element_type=F32,
                                              precision=hi) + b1_ref[...]))
        h2_scr[...] = jnp.sin(freq_ref[...] * (jnp.dot(w2T_ref[...], h1, preferred_element_type=F32,
                                                       precision=hi) + b2_ref[...]))
    window = jnp.exp(-delta_ref[...] * tn_ref[...])
    taps = []
    for d in range(2):
        h3 = jnp.dot(w3T_ref[0, d], h2_scr[...], preferred_element_type=F32, precision=hi) + b3_ref[0, d]
        taps.append(h3 * window)
    lane = lax.broadcasted_iota(jnp.int32, taps[1].shape, 1)
    taps[1] = jnp.where(lane == 0, 0.0, taps[1])
    ssq = jnp.sum(taps[0] * taps[0] + taps[1] * taps[1], axis=1, keepdims=True)
    scale = lax.rsqrt(ssq + EPS)
    o_ref[0, 0] = taps[0] * scale
    o_ref[0, 1] = taps[1] * scale


def hyena_taps(L, w1, b1, freq, w2, b2, w3, b3):
    C = HY_WIDTH
    cb = HY_TAPS_CB
    featT, tn = hyena_features(L)
    col = lambda a: a.reshape(-1, 1).astype(F32)
    w3T = w3.T.reshape(HY_ORDER, 2, C, HY_HIDDEN).astype(F32)
    b3c = b3.reshape(HY_ORDER, 2, C, 1).astype(F32)
    full = lambda a: pl.BlockSpec(a.shape, lambda i, o: (0,) * a.ndim)
    args = [featT, tn, w1.T.astype(F32), col(b1), w2.T.astype(F32), col(b2), col(freq)]
    return pl.pallas_call(
        _hy_taps_body,
        grid=(C // cb, HY_ORDER),
        in_specs=[full(a) for a in args]
                 + [pl.BlockSpec((1, 2, cb, HY_HIDDEN), lambda i, o: (o, 0, i, 0)),
                    pl.BlockSpec((1, 2, cb, 1), lambda i, o: (o, 0, i, 0)),
                    pl.BlockSpec((cb, 1), lambda i, o: (i, 0))],
        out_specs=pl.BlockSpec((1, 2, cb, L), lambda i, o: (o, 0, i, 0)),
        out_shape=jax.ShapeDtypeStruct((HY_ORDER, 2, C, L), F32),
        scratch_shapes=[pltpu.VMEM((HY_HIDDEN, L), F32)],
        compiler_params=pltpu.CompilerParams(dimension_semantics=("arbitrary", "arbitrary"),
                                             vmem_limit_bytes=VMEM_LIMIT),
        name="hyena_taps",
    )(*args, w3T, b3c, col(hyena_deltas()))


def _hy_spec_body(t_ref, ma_ref, mb_ref, tr_ref, ti_ref, o_ref, z_scr, y_scr, ya_scr, x_scr):
    nch = t_ref.shape[2]
    half = FFT_R // 2
    z_scr[half:FFT_R, :] = jnp.zeros((half, 2 * nch * FFT_R), F32)
    for d in range(2):
        for c in range(nch):
            z_scr[0:half, _chan(d * nch + c)] = t_ref[0, d, c]
    _fft_forward(z_scr, y_scr, ya_scr, x_scr, ma_ref, mb_ref, tr_ref, ti_ref, 2 * nch)
    n = nch * FFT_R
    o_ref[0, :, :, 0:FFT_R] = (x_scr[0:n, 0:FFT_R] + x_scr[n:2 * n, 0:FFT_R]).reshape(nch, FFT_R, FFT_R)
    o_ref[0, :, :, FFT_R:2 * FFT_R] = (x_scr[0:n, FFT_R:2 * FFT_R]
                                       - x_scr[n:2 * n, FFT_R:2 * FFT_R]).reshape(nch, FFT_R, FFT_R)


def hyena_spectra(taps):
    C = HY_WIDTH
    cb = HY_CB
    ma, mb, _, _, tr, ti = _dft_constants()
    t5 = taps.reshape(HY_ORDER, 2, C, FFT_R // 2, FFT_R)
    full = lambda a: pl.BlockSpec(a.shape, lambda o, i: (0,) * a.ndim)
    return pl.pallas_call(
        _hy_spec_body,
        grid=(HY_ORDER, C // cb),
        in_specs=[pl.BlockSpec((1, 2, cb, FFT_R // 2, FFT_R), lambda o, i: (o, 0, i, 0, 0))]
                 + [full(a) for a in (ma, mb, tr, ti)],
        out_specs=pl.BlockSpec((1, cb, FFT_R, 2 * FFT_R), lambda o, i: (o, i, 0, 0)),
        out_shape=jax.ShapeDtypeStruct((HY_ORDER, C, FFT_R, 2 * FFT_R), F32),
        scratch_shapes=[pltpu.VMEM((FFT_R, 2 * cb * FFT_R), F32),
                        pltpu.VMEM((2 * FFT_R, 2 * cb * FFT_R), F32),
                        pltpu.VMEM((2 * cb * FFT_R, 2 * FFT_R), BF),
                        pltpu.VMEM((2 * cb * FFT_R, 2 * FFT_R), F32)],
        compiler_params=pltpu.CompilerParams(dimension_semantics=("parallel", "parallel"),
                                             vmem_limit_bytes=VMEM_LIMIT),
        name="hyena_spectra",
    )(t5, ma, mb, tr, ti)


def hyena_long(phT, lp):
    W, T = phT.shape
    assert T == 2 * FFT_L
    taps = hyena_taps(FFT_L, lp['hy_w1'], lp['hy_b1'], lp['hy_freq'], lp['hy_w2'], lp['hy_b2'],
                      lp['hy_w3'], lp['hy_b3'])
    spectra = hyena_spectra(taps)
    z3 = hyena_conv(phT.reshape(W, FFT_R, FFT_R), spectra, lp['hy_short_w'], lp['hy_short_b'], lp['hy_bias'])
    return z3.reshape(HY_WIDTH, T)


CONF_TL = 512
CONF_HALO = 16
SUBLANES = 8
CONF_ROWS = 32


def _conf_body(a_ref, b_ref, ap_ref, bp_ref, an_ref, bn_ref, w_ref, cb_ref, lg_ref, lb_ref, o_ref, h_scr, sh_scr,
               *, tiles_per_seq):
    tl = a_ref.shape[0]
    i = pl.program_id(0) % tiles_per_seq
    glu = lambda a, b: a * jax.nn.sigmoid(b)
    h_scr[CONF_HALO:CONF_HALO + tl, :] = glu(a_ref[...], b_ref[...])
    h_scr[0:CONF_HALO, :] = jnp.where(i == 0, 0.0, glu(ap_ref[...], bp_ref[...]))
    h_scr[CONF_HALO + tl:2 * CONF_HALO + tl, :] = jnp.where(i == tiles_per_seq - 1, 0.0,
                                                             glu(an_ref[...], bn_ref[...]))
    left = (CONF_TAPS - 1) // 2
    span = sh_scr.shape[1]
    for s in range(SUBLANES):
        sh_scr[s] = h_scr[s:s + span, :]
    for r0 in range(0, tl, CONF_ROWS):
        acc = jnp.zeros((CONF_ROWS, CONF_WIDTH), F32)
        for k in range(CONF_TAPS):
            start = CONF_HALO - left + r0 + k
            s = start % SUBLANES
            acc = acc + w_ref[k:k + 1, :] * sh_scr[s, start - s:start - s + CONF_ROWS, :]
        h = acc + cb_ref[...]
        mu = jnp.mean(h, axis=-1, keepdims=True)
        d = h - mu
        var = jnp.mean(d * d, axis=-1, keepdims=True)
        hn = d * lax.rsqrt(var + EPS) * lg_ref[...] + lb_ref[...]
        o_ref[r0:r0 + CONF_ROWS, :] = (hn * jax.nn.sigmoid(hn)).astype(o_ref.dtype)


def conformer_pre(p2, B, L, col_a, col_b, dw_w, dw_b, ln_g, ln_b):
    W = CONF_WIDTH
    tl = min(CONF_TL, L)
    tps = L // tl
    hb = tl // CONF_HALO
    nhb = B * L // CONF_HALO
    main = lambda col: pl.BlockSpec((tl, W), lambda i: (i, col))
    prev = lambda col: pl.BlockSpec((CONF_HALO, W), lambda i: (jnp.maximum(i * hb - 1, 0), col))
    nxt = lambda col: pl.BlockSpec((CONF_HALO, W), lambda i: (jnp.minimum((i + 1) * hb, nhb - 1), col))
    row = lambda a: a.reshape(1, W).astype(F32)
    vec = pl.BlockSpec((1, W), lambda i: (0, 0))
    return pl.pallas_call(
        functools.partial(_conf_body, tiles_per_seq=tps),
        grid=(B * tps,),
        in_specs=[main(col_a), main(col_b), prev(col_a), prev(col_b), nxt(col_a), nxt(col_b),
                  pl.BlockSpec((CONF_TAPS, W), lambda i: (0, 0)), vec, vec, vec],
        out_specs=pl.BlockSpec((tl, W), lambda i: (i, 0)),
        out_shape=jax.ShapeDtypeStruct((B * L, W), BF),
        scratch_shapes=[pltpu.VMEM((tl + 2 * CONF_HALO, W), F32),
                        pltpu.VMEM((SUBLANES, tl + 2 * CONF_HALO - SUBLANES, W), F32)],
        compiler_params=pltpu.CompilerParams(dimension_semantics=("parallel",), vmem_limit_bytes=VMEM_LIMIT),
        name="conformer_pre",
    )(p2, p2, p2, p2, p2, p2, dw_w.astype(F32), row(dw_b), row(ln_g), row(ln_b))


MERGE_TM = 512


def _merge_body(x_ref, ret_ref, conf_ref, hy_ref, g0_ref, g1_ref, g2_ref, g3_ref, g4_ref, g5_ref, gate_ref,
                wr_ref, wc_ref, wh_ref, wo_ref, o_ref, *, hy_transposed):
    half = D_MODEL // 2
    b_ret = jnp.dot(ret_ref[...], wr_ref[...], preferred_element_type=F32)
    b_conf = jnp.dot(conf_ref[...], wc_ref[...], preferred_element_type=F32)
    if hy_transposed:
        b_hy = lax.dot_general(hy_ref[...].astype(BF), wh_ref[...], (((0,), (0,)), ((), ())),
                               preferred_element_type=F32)
    else:
        b_hy = jnp.dot(hy_ref[...].astype(BF), wh_ref[...], preferred_element_type=F32)
    sig = jax.nn.sigmoid
    lo = (sig(g0_ref[...]) * b_ret[:, :half] + sig(g2_ref[...]) * b_conf[:, :half]
          + sig(g4_ref[...]) * b_hy[:, :half])
    hi = (sig(g1_ref[...]) * b_ret[:, half:] + sig(g3_ref[...]) * b_conf[:, half:]
          + sig(g5_ref[...]) * b_hy[:, half:])
    m = jnp.concatenate([lo, hi], axis=1).astype(BF)
    y = jnp.dot(m, wo_ref[...], preferred_element_type=F32)
    o_ref[...] = x_ref[...] + gate_ref[0] * y


def merge_branches(x, p2, col_gate, ret_pre, conf_pre, hy, gate, w_ret, w_conf, w_hy, w_out, hy_transposed):
    B, L, D = x.shape
    T = B * L
    tm = min(MERGE_TM, L)
    tps = L // tm
    half = D // 2
    per_batch = gate.shape[0] > 1
    gspec = lambda k: pl.BlockSpec((tm, half), lambda i: (i, col_gate + k))
    full = lambda a: pl.BlockSpec(a.shape, lambda i: (0, 0))
    hy_spec = (pl.BlockSpec((hy.shape[0], tm), lambda i: (0, i)) if hy_transposed
               else pl.BlockSpec((tm, hy.shape[1]), lambda i: (i, 0)))
    ws = [w.astype(BF) for w in (w_ret, w_conf, w_hy, w_out)]
    out = pl.pallas_call(
        functools.partial(_merge_body, hy_transposed=hy_transposed),
        grid=(T // tm,),
        in_specs=[pl.BlockSpec((tm, D), lambda i: (i, 0)),
                  pl.BlockSpec((tm, ret_pre.shape[1]), lambda i: (i, 0)),
                  pl.BlockSpec((tm, conf_pre.shape[1]), lambda i: (i, 0)),
                  hy_spec] + [gspec(k) for k in range(2 * N_BRANCH)]
                 + [pl.BlockSpec((1, 1, D), lambda i: (i // tps if per_batch else 0, 0, 0))]
                 + [full(w) for w in ws],
        out_specs=pl.BlockSpec((tm, D), lambda i: (i, 0)),
        out_shape=jax.ShapeDtypeStruct((T, D), F32),
        compiler_params=pltpu.CompilerParams(dimension_semantics=("parallel",), vmem_limit_bytes=VMEM_LIMIT),
        name="merge_branches",
    )(x.reshape(T, D), ret_pre, conf_pre, hy, p2, p2, p2, p2, p2, p2, gate, *ws)
    return out.reshape(B, L, D)


def mixer_block(x, p, phT, ret_pre, gate, lp):
    B, L, _ = p.shape
    p2 = p.reshape(B * L, -1)
    half = D_MODEL // 2
    conf_pre = conformer_pre(p2, B, L, P_CONF // half, P_CONF // half + 1,
                             lp['conf_dw_w'], lp['conf_dw_b'], lp['conf_ln_g'], lp['conf_ln_b'])
    long_seq = L == FFT_L
    if long_seq:
        hy = hyena_long(phT, lp)
    else:
        hy = hyena_branch(phT.T.reshape(B, L, HY_COLS), hyena_filters(L, lp), lp).reshape(B * L, HY_WIDTH)
    return merge_branches(x, p2, P_GATE // half, ret_pre, conf_pre, hy, gate,
                          lp['w_br_ret'], lp['w_br_conf'], lp['w_br_hy'], lp['w_out'], long_seq)


LOG2E = 1.4426950408889634
NEG_BIG = -1e30
PEER_ROUTE_TM = 512
PEER_TM = 512
PEER_TE = 2048
PEER_ROWS = PEER_TE // PEER_NKEYS
PEER_CHUNK = 256
LANES = 128


def _batcher_pairs(lo, hi):
    def merge(lo, hi, r):
        step = r * 2
        if step < hi - lo:
            yield from merge(lo, hi, step)
            yield from merge(lo + r, hi, step)
            yield from [(i, i + r) for i in range(lo + r, hi - r, step)]
        else:
            yield (lo, lo + r)
    if hi - lo >= 1:
        mid = lo + (hi - lo) // 2
        yield from _batcher_pairs(lo, mid)
        yield from _batcher_pairs(mid + 1, hi)
        yield from merge(lo, hi, 1)


SORT16 = tuple(_batcher_pairs(0, PEER_TOPK - 1))


def _cmpx(v, i, j):
    a, b = v[i], v[j]
    v[i] = jnp.maximum(a, b)
    v[j] = jnp.minimum(a, b)


def _bitonic_desc(v):
    for dist in (8, 4, 2, 1):
        for i in range(PEER_TOPK):
            if not i & dist:
                _cmpx(v, i, i + dist)


def _merge_top(cur, other, dropped):
    out = list(cur)
    for r, o in enumerate(other):
        i = PEER_TOPK - 1 - r
        out[i] = jnp.maximum(cur[i], o)
        dropped = jnp.maximum(dropped, jnp.minimum(cur[i], o))
    _bitonic_desc(out)
    return out, dropped


def _top17_of_rows(s):
    v = [s[8 * k:8 * k + 8, :] for k in range(PEER_NKEYS // 8)]
    for i, j in SORT16:
        _cmpx(v, i, j)
    dropped = jnp.full(v[0].shape, NEG_BIG, jnp.float32)
    for shift in (4, 2, 1):
        other = [pltpu.roll(x, shift, 0) for x in v]
        dropped = jnp.maximum(dropped, pltpu.roll(dropped, shift, 0))
        v, dropped = _merge_top(v, other, dropped)
    return v + [dropped]


def _peer_route_body(hT_ref, wqT_ref, keys_ref, s1p_ref, thr_ref, s2_ref):
    K = PEER_TOPK
    half = PEER_DKEY // 2
    qT = jnp.dot(wqT_ref[...], hT_ref[...], preferred_element_type=jnp.float32)
    s1 = jnp.dot(keys_ref[0], qT[:half], preferred_element_type=jnp.float32,
                 precision=lax.Precision.HIGHEST)
    s2 = jnp.dot(keys_ref[1], qT[half:], preferred_element_type=jnp.float32,
                 precision=lax.Precision.HIGHEST)
    t1 = _top17_of_rows(s1)
    t2 = _top17_of_rows(s2)
    best = [t1[0] + t2[b] for b in range(K)]
    dropped = t1[0] + t2[K]
    for a in range(1, 8):
        best, dropped = _merge_top(best, [t1[a] + t2[b] for b in range((K + 1) // (a + 1))], dropped)
    best, dropped = _merge_top(best, [t1[a] + t2[0] for a in range(8, K + 1)], dropped)
    tau = 0.5 * (best[K - 1] + dropped)
    z = jnp.zeros_like(tau)
    for r in range(K):
        z = z + jnp.exp(best[r] - best[0])
    lse = best[0] + jnp.log(z)
    m2 = t2[0][0:1]
    s1p_ref[0] = jnp.exp2((s1 + (m2 - lse[0:1])) * LOG2E)
    thr_ref[0] = jnp.exp2((tau[0:1] - m2 - s1) * LOG2E)
    s2_ref[0] = jnp.exp2((s2 - m2) * LOG2E)


def peer_route(hT, wqT, keys):
    D, T = hT.shape
    tm = min(PEER_ROUTE_TM, T)
    assert T % tm == 0
    shp = jax.ShapeDtypeStruct((PEER_HEADS, PEER_NKEYS, T), jnp.float32)
    ospec = pl.BlockSpec((1, PEER_NKEYS, tm), lambda t, h: (h, 0, t))
    return pl.pallas_call(
        _peer_route_body,
        grid=(T // tm, PEER_HEADS),
        in_specs=[pl.BlockSpec((D, tm), lambda t, h: (0, t)),
                  pl.BlockSpec((PEER_DKEY, D), lambda t, h: (h, 0)),
                  pl.BlockSpec((2, PEER_NKEYS, PEER_DKEY // 2), lambda t, h: (h, 0, 0))],
        out_specs=[ospec, ospec, ospec],
        out_shape=[shp, shp, shp],
        compiler_params=pltpu.CompilerParams(dimension_semantics=("parallel", "arbitrary")),
        name="peer_route",
    )(hT, wqT, keys.reshape(PEER_HEADS * 2, PEER_NKEYS, PEER_DKEY // 2))


def _gelu_tanh(a):
    inner = a * (0.7978845608028654 + 0.035677408136300125 * (a * a))
    half = 0.5 * a
    return half + half * jnp.tanh(inner)


def _peer_expert_body(u_ref, hT_ref, vT_ref, s1p_ref, thr_ref, s2_ref, x_ref, gate_ref, o_ref,
                      a_scr, w_scr, acc_scr):
    tm = hT_ref.shape[1]

    @pl.when(pl.program_id(1) == 0)
    def _():
        acc_scr[...] = jnp.zeros_like(acc_scr)

    for q in range(PEER_TE // PEER_CHUNK):
        erows = slice(q * PEER_CHUNK, (q + 1) * PEER_CHUNK)
        a_scr[erows, :] = jnp.dot(u_ref[erows, :], hT_ref[...], preferred_element_type=jnp.float32)
    for q in range(PEER_TE // PEER_CHUNK):
        erows = slice(q * PEER_CHUNK, (q + 1) * PEER_CHUNK)
        for r in range(q * PEER_CHUNK // PEER_NKEYS, (q + 1) * PEER_CHUNK // PEER_NKEYS):
            for g in range(tm // LANES):
                lanes = pl.ds(g * LANES, LANES)
                s1b = [jnp.broadcast_to(s1p_ref[h, r:r + 1, lanes], (16, LANES))
                       for h in range(PEER_HEADS)]
                thb = [jnp.broadcast_to(thr_ref[h, r:r + 1, lanes], (16, LANES))
                       for h in range(PEER_HEADS)]
                for jb in range(PEER_NKEYS // 16):
                    rows = pl.ds(r * PEER_NKEYS + jb * 16, 16)
                    gate = jnp.zeros((16, LANES), jnp.float32)
                    for h in range(PEER_HEADS):
                        e2 = s2_ref[h, jb * 16:(jb + 1) * 16, lanes]
                        gate = gate + jnp.where(e2 >= thb[h], e2 * s1b[h], 0.0)
                    w_scr[rows, lanes] = (_gelu_tanh(a_scr[rows, lanes]) * gate).astype(jnp.bfloat16)
        acc_scr[...] += jnp.dot(vT_ref[:, erows], w_scr[erows, :], preferred_element_type=jnp.float32)

    @pl.when(pl.program_id(1) == pl.num_programs(1) - 1)
    def _():
        o_ref[...] = x_ref[...] + gate_ref[0] * acc_scr[...].T


def peer_experts(x2, gate, tiles_per_seq, hT, u_bf, vT_bf, s1p, thr, s2):
    D, T = hT.shape
    E = u_bf.shape[0]
    tm = min(PEER_TM, T)
    te = PEER_TE
    assert T % tm == 0 and E % te == 0
    per_batch = gate.shape[0] > 1
    return pl.pallas_call(
        _peer_expert_body,
        grid=(T // tm, E // te),
        in_specs=[pl.BlockSpec((te, D), lambda t, e: (e, 0)),
                  pl.BlockSpec((D, tm), lambda t, e: (0, t)),
                  pl.BlockSpec((D, te), lambda t, e: (0, e)),
                  pl.BlockSpec((PEER_HEADS, PEER_ROWS, tm), lambda t, e: (0, e, t)),
                  pl.BlockSpec((PEER_HEADS, PEER_ROWS, tm), lambda t, e: (0, e, t)),
                  pl.BlockSpec((PEER_HEADS, PEER_NKEYS, tm), lambda t, e: (0, 0, t)),
                  pl.BlockSpec((tm, D), lambda t, e: (t, 0)),
                  pl.BlockSpec((1, 1, D), lambda t, e: (t // tiles_per_seq if per_batch else 0, 0, 0))],
        out_specs=pl.BlockSpec((tm, D), lambda t, e: (t, 0)),
        out_shape=jax.ShapeDtypeStruct((T, D), jnp.float32),
        scratch_shapes=[pltpu.VMEM((te, tm), jnp.float32),
                        pltpu.VMEM((te, tm), jnp.bfloat16),
                        pltpu.VMEM((D, tm), jnp.float32)],
        compiler_params=pltpu.CompilerParams(dimension_semantics=("parallel", "arbitrary"),
                                             vmem_limit_bytes=VMEM_LIMIT),
        name="peer_experts",
    )(u_bf, hT, vT_bf, s1p, thr, s2, x2, gate)


def _peer_in_body(x_ref, g_ref, shift_ref, scale_ref, o_ref):
    x = x_ref[...]
    y = x * lax.rsqrt(jnp.mean(x * x, axis=-1, keepdims=True) + EPS) * g_ref[...]
    o_ref[...] = (y * (1.0 + scale_ref[0]) + shift_ref[0]).T.astype(jnp.bfloat16)


def peer_input(x2, gain, shift, scale, tiles_per_seq):
    T, D = x2.shape
    tm = min(PEER_TM, T)
    per_batch = shift.shape[0] > 1
    mod_spec = pl.BlockSpec((1, 1, D), lambda t: (t // tiles_per_seq if per_batch else 0, 0, 0))
    return pl.pallas_call(
        _peer_in_body,
        grid=(T // tm,),
        in_specs=[pl.BlockSpec((tm, D), lambda t: (t, 0)), pl.BlockSpec((1, D), lambda t: (0, 0)),
                  mod_spec, mod_spec],
        out_specs=pl.BlockSpec((D, tm), lambda t: (0, t)),
        out_shape=jax.ShapeDtypeStruct((D, T), jnp.bfloat16),
        compiler_params=pltpu.CompilerParams(dimension_semantics=("parallel",)),
        name="peer_input",
    )(x2, gain.reshape(1, D).astype(jnp.float32), shift, scale)


def peer_ffn(x, gain, shift, scale, gate, w_q, sub_keys, u_bf, v_bf):
    B, L, D = x.shape
    x2 = x.reshape(B * L, D)
    tiles_per_seq = max(L // min(PEER_TM, B * L), 1)
    hT = peer_input(x2, gain, shift, scale, tiles_per_seq)
    s1p, thr, s2 = peer_route(hT, w_q.T.astype(jnp.bfloat16), sub_keys.astype(jnp.float32))
    return peer_experts(x2, gate, tiles_per_seq, hT, u_bf, v_bf, s1p, thr, s2).reshape(B, L, D)


def kernel(x, c, ctx, c_ctx, ada_w, ada_b, norm1_g, norm2_g, w_in, ret_decay_logit, ret_gn_g, w_br_ret, conf_dw_w, conf_dw_b, conf_ln_g, conf_ln_b, w_br_conf, hy_short_w, hy_short_b, hy_w1, hy_b1, hy_freq, hy_w2, hy_b2, hy_w3, hy_b3, hy_bias, w_br_hy, w_out, peer_wq, peer_keys, peer_u, peer_v, final_g):
    depth = ada_w.shape[0]
    B, L, _ = x.shape
    Lc = ctx.shape[1]
    HK = RET_HEADS * RET_DK
    HV = RET_HEADS * RET_DV
    ret_cols = dict(col_q=P_Q // HK, col_k=P_K // HK, col_qp=P_QP // HK, col_kp=P_KP // HK, col_v=P_V // HV)
    w_in_bf = w_in.astype(BF)
    cosf, sinf = rope_tables(*grid_rope_angles(L // GRID_W))
    cosf_c, sinf_c = jnp.ones((Lc, HK), F32), jnp.zeros((Lc, HK), F32)
    zero_state = jnp.zeros((B, RET_HEADS, 2 * RET_DK, RET_DV), F32)

    def retention(p, seq, cos_t, sin_t, tabs, s0_f, s0_b, gn_g):
        p2 = p.reshape(B * seq, -1)
        o_f, fin_f = retention_pass(p2, B, seq, cos_t, sin_t, tabs[0], s0_f, backward=False, **ret_cols)
        pre, fin_b = retention_pass(p2, B, seq, cos_t, sin_t, tabs[1], s0_b, backward=True, o_fwd=o_f,
                                    col_g=P_G // HV, gn_g=gn_g, **ret_cols)
        return pre, fin_f, fin_b

    for l in range(depth):
        last = l == depth - 1
        lp = {
            'log_g': jax.nn.log_sigmoid(ret_decay_logit[l].astype(jnp.float32)),
            'ret_gn_g': ret_gn_g[l], 'w_br_ret': w_br_ret[l],
            'conf_dw_w': conf_dw_w[l], 'conf_dw_b': conf_dw_b[l],
            'conf_ln_g': conf_ln_g[l], 'conf_ln_b': conf_ln_b[l], 'w_br_conf': w_br_conf[l],
            'hy_short_w': hy_short_w[l], 'hy_short_b': hy_short_b[l],
            'hy_w1': hy_w1[l], 'hy_b1': hy_b1[l], 'hy_freq': hy_freq[l],
            'hy_w2': hy_w2[l], 'hy_b2': hy_b2[l], 'hy_w3': hy_w3[l], 'hy_b3': hy_b3[l],
            'hy_bias': hy_bias[l], 'w_br_hy': w_br_hy[l], 'w_out': w_out[l],
        }
        u_bf = peer_u[l].astype(jnp.bfloat16)
        v_bf = peer_v[l].T.astype(jnp.bfloat16)
        mod = adaln(c, ada_w[l], ada_b[l])
        mod_c = adaln(c_ctx[None, :], ada_w[l], ada_b[l])

        wl = w_in_bf[l]
        w_tok = jnp.concatenate([wl[:, :OFF_HY], wl[:, OFF_GATE:], rope_partner_columns(wl[:, OFF_Q:OFF_K]),
                                 rope_partner_columns(wl[:, OFF_K:OFF_V])], axis=1)
        w_hyT = wl[:, OFF_HY:OFF_GATE].T
        tabs = (retention_tables(lp['log_g'][0], False), retention_tables(lp['log_g'][1], True))

        p_c, phT_c = norm_project(ctx, norm1_g[l], mod_c[0], mod_c[1], w_tok, w_hyT)
        pre_c, s_f, s_b = retention(p_c, Lc, cosf_c, sinf_c, tabs, zero_state, zero_state, lp['ret_gn_g'])

        p, phT = norm_project(x, norm1_g[l], mod[0], mod[1], w_tok, w_hyT)
        pre, _, _ = retention(p, L, cosf, sinf, tabs, s_f, s_b, lp['ret_gn_g'])
        x = mixer_block(x, p, phT, pre, mod[2], lp)

        x = peer_ffn(x, norm2_g[l], mod[3], mod[4], mod[5], peer_wq[l], peer_keys[l], u_bf, v_bf)

        if not last:
            ctx = mixer_block(ctx, p_c, phT_c, pre_c, mod_c[2], lp)
            ctx = peer_ffn(ctx, norm2_g[l], mod_c[3], mod_c[4], mod_c[5], peer_wq[l], peer_keys[l], u_bf, v_bf)
    return rms_norm(x, final_g)
```
